```python
import math
import jax, jax.numpy as jnp
from jax import lax
import numpy as np

D_MODEL = 2048
BATCH = 1
SEQ = 16384
DEPTH = 1

N_MEM = 256
EPS = 1e-6

DA_HEADS = 8
DA_QK_DIM = 64
DA_V_DIM = 2 * DA_QK_DIM
DA_ROT_DIM = DA_QK_DIM // 4
ROPE_THETA = 500000.0
Q_BLOCK = 128

RET_HEADS = 8
RET_QK_DIM = 64
RET_V_DIM = 128
RET_ROT_BASE = 10000.0
RET_CHUNK = 128

MEM_HEADS = 4
MEM_HEAD_DIM = 256

D_FF = 256 * (-(-(8 * D_MODEL) // (3 * 256)))

DA_Q_W = DA_HEADS * 2 * DA_QK_DIM
DA_K_W = DA_HEADS * 2 * DA_QK_DIM
DA_V_W = DA_HEADS * DA_V_DIM
RET_Q_W = RET_HEADS * RET_QK_DIM
RET_K_W = RET_HEADS * RET_QK_DIM
RET_V_W = RET_HEADS * RET_V_DIM
RET_G_W = RET_HEADS * RET_V_DIM
MEM_W = MEM_HEADS * MEM_HEAD_DIM
N_BRANCH = 3
GATE_W = N_BRANCH * D_MODEL
IN_SIZES = (DA_Q_W, DA_K_W, DA_V_W, RET_Q_W, RET_K_W, RET_V_W, RET_G_W, MEM_W, GATE_W)
IN_COLS = DA_Q_W + DA_K_W + DA_V_W + RET_Q_W + RET_K_W + RET_V_W + RET_G_W + MEM_W + GATE_W

kernel_name = 'hybrid_diffattn_retention_memxattn_swiglu'


def _rms_norm(x, g):
    xf = x.astype(jnp.float32)
    out = xf * lax.rsqrt(jnp.mean(xf * xf, axis=-1, keepdims=True) + EPS)
    return (out * g.astype(jnp.float32)).astype(x.dtype)


def _rope(x, positions, rot_dim, theta):
    half = rot_dim // 2
    inv_freq = theta ** (-jnp.arange(half, dtype=jnp.float32) * 2.0 / rot_dim)
    ang = positions.astype(jnp.float32)[..., None] * inv_freq
    cos = jnp.cos(ang)[:, :, None, :]
    sin = jnp.sin(ang)[:, :, None, :]
    xf = x.astype(jnp.float32)
    x1 = xf[..., :half]
    x2 = xf[..., half:rot_dim]
    out = jnp.concatenate([x1 * cos - x2 * sin, x2 * cos + x1 * sin, xf[..., rot_dim:]], axis=-1)
    return out.astype(x.dtype)


def _split_cols(t, sizes):
    outs, start = [], 0
    for s in sizes:
        outs.append(t[..., start:start + s])
        start += s
    return outs


def _diff_attention(q, k, v, q_g, k_g, lq1, lk1, lq2, lk2, subln_g, positions, lambda_init):
    B, S, _ = q.shape
    H, d, dv = DA_HEADS, DA_QK_DIM, DA_V_DIM
    q = _rms_norm(q.reshape(B, S, H * 2, d), q_g)
    k = _rms_norm(k.reshape(B, S, H * 2, d), k_g)
    q = _rope(q, positions, DA_ROT_DIM, ROPE_THETA).astype(jnp.float32)
    k = _rope(k, positions, DA_ROT_DIM, ROPE_THETA).astype(jnp.float32)
    v = v.reshape(B, S, H, dv).astype(jnp.float32)
    lam = (jnp.exp(jnp.sum(lq1.astype(jnp.float32) * lk1.astype(jnp.float32)))
           - jnp.exp(jnp.sum(lq2.astype(jnp.float32) * lk2.astype(jnp.float32))) + lambda_init)
    scale = d ** -0.5
    nb = S // Q_BLOCK
    qb = q.reshape(B, nb, Q_BLOCK, H, 2, d).transpose(1, 0, 3, 4, 2, 5)
    kh = k.reshape(B, S, H, 2, d).transpose(0, 2, 3, 1, 4)
    vh = v.transpose(0, 2, 1, 3)
    key_pos = jnp.arange(S)

    def attend_block(args):
        q_blk, blk = args
        s = jnp.einsum('bhmqd,bhmkd->bhmqk', q_blk, kh) * scale
        q_pos = blk * Q_BLOCK + jnp.arange(Q_BLOCK)
        causal = key_pos[None, :] <= q_pos[:, None]
        s = jnp.where(causal, s, -jnp.inf)
        p = jax.nn.softmax(s, axis=-1)
        a = p[:, :, 0] - lam * p[:, :, 1]
        return jnp.einsum('bhqk,bhkv->bhqv', a, vh)

    o = lax.map(attend_block, (qb, jnp.arange(nb)))
    o = o.transpose(1, 0, 3, 2, 4).reshape(B, S, H, dv)
    o = _rms_norm(o, subln_g) * (1.0 - lambda_init)
    return o.reshape(B, S, H * dv)


def _retention(q, k, v, g, norm_g, positions):
    B, S, _ = q.shape
    H, dk, dv, C = RET_HEADS, RET_QK_DIM, RET_V_DIM, RET_CHUNK
    q = _rope(q.reshape(B, S, H, dk), positions, dk, RET_ROT_BASE).astype(jnp.float32)
    k = _rope(k.reshape(B, S, H, dk), positions, dk, RET_ROT_BASE).astype(jnp.float32) * (dk ** -0.5)
    v = v.reshape(B, S, H, dv).astype(jnp.float32)
    log_gamma = jnp.log(1.0 - 2.0 ** (-5.0 - jnp.arange(H, dtype=jnp.float32)))
    idx = jnp.arange(C, dtype=jnp.float32)
    rel = idx[:, None] - idx[None, :]
    inner_decay = jnp.where(rel >= 0, jnp.exp(log_gamma[:, None, None] * jnp.maximum(rel, 0.0)), 0.0)
    query_decay = jnp.exp(log_gamma[:, None] * (idx + 1.0))
    key_decay = jnp.exp(log_gamma[:, None] * (C - 1.0 - idx))
    chunk_decay = jnp.exp(log_gamma * C)
    nc = S // C

    def to_chunks(t):
        return t.reshape(B, nc, C, H, t.shape[-1]).transpose(1, 0, 3, 2, 4)

    def step(state, inp):
        qc, kc, vc = inp
        scores = jnp.einsum('bhnd,bhmd->bhnm', qc, kc) * inner_decay
        o_in = jnp.einsum('bhnm,bhmv->bhnv', scores, vc)
        o_cross = jnp.einsum('bhnd,bhdv->bhnv', qc, state) * query_decay[:, :, None]
        state = state * chunk_decay[:, None, None] + jnp.einsum('bhmd,bhmv->bhdv', kc * key_decay[:, :, None], vc)
        return state, o_in + o_cross

    state0 = jnp.zeros((B, H, dk, dv), jnp.float32)
    _, o = lax.scan(step, state0, (to_chunks(q), to_chunks(k), to_chunks(v)))
    o = o.transpose(1, 0, 3, 2, 4).reshape(B, S, H, dv)
    o = _rms_norm(o, norm_g).reshape(B, S, H * dv)
    return jax.nn.silu(g.astype(jnp.float32)) * o


def _memory_attention(q, mem, mem_norm_g, w_mem_kv, q_g, k_g):
    B, S, _ = q.shape
    M = mem.shape[1]
    H, d = MEM_HEADS, MEM_HEAD_DIM
    m = _rms_norm(mem, mem_norm_g)
    kv = m @ w_mem_kv
    k, v = kv[..., :MEM_W], kv[..., MEM_W:]
    q = _rms_norm(q.reshape(B, S, H, d), q_g).astype(jnp.float32)
    k = _rms_norm(k.reshape(B, M, H, d), k_g).astype(jnp.float32)
    v = v.reshape(B, M, H, d).astype(jnp.float32)
    s = jnp.einsum('bshd,bmhd->bhsm', q, k) * (d ** -0.5)
    p = jax.nn.softmax(s, axis=-1)
    o = jnp.einsum('bhsm,bmhd->bshd', p, v)
    return o.reshape(B, S, H * d)


def setup_inputs(seed: int = 0) -> dict:
    key = jax.random.key(seed)
    ks = jax.random.split(key, 32)
    f32 = jnp.float32

    def w(k, shape, fan_in):
        return jax.random.normal(k, shape, f32) * (fan_in ** -0.5)

    def gain(k, shape):
        return 1.0 + 0.02 * jax.random.normal(k, shape, f32)

    L, D = DEPTH, D_MODEL
    return {
        'x': jax.random.normal(ks[0], (BATCH, SEQ, D), f32),
        'mem': jax.random.normal(ks[1], (BATCH, N_MEM, D), f32),
        'positions': jnp.broadcast_to(jnp.arange(SEQ, dtype=jnp.int32)[None, :], (BATCH, SEQ)),
        'attn_norm_g': gain(ks[2], (L, D)),
        'w_in': w(ks[3], (L, D, IN_COLS), D),
        'da_q_norm_g': gain(ks[4], (L, DA_QK_DIM)),
        'da_k_norm_g': gain(ks[5], (L, DA_QK_DIM)),
        'da_lambda_q1': 0.1 * jax.random.normal(ks[6], (L, DA_QK_DIM), f32),
        'da_lambda_k1': 0.1 * jax.random.normal(ks[7], (L, DA_QK_DIM), f32),
        'da_lambda_q2': 0.1 * jax.random.normal(ks[8], (L, DA_QK_DIM), f32),
        'da_lambda_k2': 0.1 * jax.random.normal(ks[9], (L, DA_QK_DIM), f32),
        'da_subln_g': gain(ks[10], (L, DA_V_DIM)),
        'ret_norm_g': gain(ks[11], (L, RET_V_DIM)),
        'mem_norm_g': gain(ks[12], (L, D)),
        'w_mem_kv': w(ks[13], (L, D, 2 * MEM_W), D),
        'mem_q_norm_g': gain(ks[14], (L, MEM_HEAD_DIM)),
        'mem_k_norm_g': gain(ks[15], (L, MEM_HEAD_DIM)),
        'w_o_da': w(ks[16], (L, DA_V_W, D), DA_V_W),
        'w_o_ret': w(ks[17], (L, RET_V_W, D), RET_V_W),
        'w_o_mem': w(ks[18], (L, MEM_W, D), MEM_W),
        'w_out': w(ks[19], (L, D, D), D),
        'ffn_norm_g': gain(ks[20], (L, D)),
        'w_ffn_gate': w(ks[21], (L, D, D_FF), D),
        'w_ffn_up': w(ks[22], (L, D, D_FF), D),
        'w_ffn_down': w(ks[23], (L, D_FF, D), D_FF),
    }


def reference(x, mem, positions, attn_norm_g, w_in, da_q_norm_g, da_k_norm_g,
              da_lambda_q1, da_lambda_k1, da_lambda_q2, da_lambda_k2, da_subln_g,
              ret_norm_g, mem_norm_g, w_mem_kv, mem_q_norm_g, mem_k_norm_g,
              w_o_da, w_o_ret, w_o_mem, w_out, ffn_norm_g, w_ffn_gate, w_ffn_up, w_ffn_down):
    for l in range(DEPTH):
        lambda_init = 0.8 - 0.6 * math.exp(-0.3 * l)
        h = _rms_norm(x, attn_norm_g[l])
        proj = h @ w_in[l]
        da_q, da_k, da_v, r_q, r_k, r_v, r_g, m_q, gates = _split_cols(proj, IN_SIZES)
        o_da = _diff_attention(da_q, da_k, da_v, da_q_norm_g[l], da_k_norm_g[l],
                               da_lambda_q1[l], da_lambda_k1[l], da_lambda_q2[l], da_lambda_k2[l],
                               da_subln_g[l], positions, lambda_init).astype(x.dtype)
        o_ret = _retention(r_q, r_k, r_v, r_g, ret_norm_g[l], positions).astype(x.dtype)
        o_mem = _memory_attention(m_q, mem, mem_norm_g[l], w_mem_kv[l],
                                  mem_q_norm_g[l], mem_k_norm_g[l]).astype(x.dtype)
        g_da, g_ret, g_mem = _split_cols(gates, (D_MODEL, D_MODEL, D_MODEL))
        merged = (jax.nn.sigmoid(g_da) * (o_da @ w_o_da[l])
                  + jax.nn.sigmoid(g_ret) * (o_ret @ w_o_ret[l])
                  + jax.nn.sigmoid(g_mem) * (o_mem @ w_o_mem[l]))
        x = x + merged @ w_out[l]
        hf = _rms_norm(x, ffn_norm_g[l])
        x = x + (jax.nn.silu(hf @ w_ffn_gate[l]) * (hf @ w_ffn_up[l])) @ w_ffn_down[l]
    return x
```

```python
import functools
import math

import jax
import jax.numpy as jnp
from jax import lax
from jax.experimental import pallas as pl
from jax.experimental.pallas import tpu as pltpu

F32 = jnp.float32
BF16 = jnp.bfloat16

D_MODEL = 2048
EPS = 1e-6

DA_HEADS = 8
DA_QK_DIM = 64
DA_V_DIM = 128
DA_ROT_DIM = 16
ROPE_THETA = 500000.0

RET_HEADS = 8
RET_QK_DIM = 64
RET_V_DIM = 128
RET_ROT_BASE = 10000.0

MEM_HEADS = 4
MEM_HEAD_DIM = 256

D_FF = 5632

DA_Q_OFF = 0
DA_K_OFF = 1024
DA_V_OFF = 2048
RET_Q_OFF = 3072
RET_K_OFF = 3584
RET_V_OFF = 4096
RET_G_OFF = 5120
MEM_Q_OFF = 6144
GATE_OFF = 7168
IN_COLS = 13312

LANES = 128
MXU_DIM = 256
VMEM_LIMIT = 60 * 1024 * 1024
NEG_BIG = -1e30

IN_TM, IN_TN = 1024, 1024
PREP_TM = 512
DA_TQ = 512
RET_CHUNK = 256
MERGE_TM, MERGE_TN = 512, 512
FFN_TM, FFN_TF = 512, 512


def _params(*sem):
    return pltpu.CompilerParams(dimension_semantics=sem, vmem_limit_bytes=VMEM_LIMIT)


def _rms(x, g):
    return x * lax.rsqrt(jnp.mean(x * x, axis=-1, keepdims=True) + EPS) * g


def _in_proj_body(x_ref, g_ref, w_ref, o_ref, h_ref):
    @pl.when(pl.program_id(1) == 0)
    def _():
        h_ref[...] = _rms(x_ref[...], g_ref[...]).astype(BF16)

    o_ref[...] = jnp.dot(h_ref[...], w_ref[...], preferred_element_type=F32).astype(o_ref.dtype)


def _in_proj(x, g, w):
    s = x.shape[0]
    return pl.pallas_call(
        _in_proj_body,
        out_shape=jax.ShapeDtypeStruct((s, IN_COLS), BF16),
        grid=(s // IN_TM, IN_COLS // IN_TN),
        in_specs=[
            pl.BlockSpec((IN_TM, D_MODEL), lambda i, j: (i, 0)),
            pl.BlockSpec((1, D_MODEL), lambda i, j: (0, 0)),
            pl.BlockSpec((D_MODEL, IN_TN), lambda i, j: (0, j)),
        ],
        out_specs=pl.BlockSpec((IN_TM, IN_TN), lambda i, j: (i, j)),
        scratch_shapes=[pltpu.VMEM((IN_TM, D_MODEL), BF16)],
        compiler_params=_params("arbitrary", "arbitrary"),
        name="in_proj",
    )(x, g, w)


def _rotate(x, cos, sin_signed, first_half, half):
    w = x.shape[-1]
    partner = jnp.where(first_half, pltpu.roll(x, w - half, 1), pltpu.roll(x, half, 1))
    return x * cos + partner * sin_signed


def _qk_prep_body(pos_ref, invf_ref, dqk_ref, rqk_ref, gain_ref, ones_ref, dqk_out, rqk_out):
    tm = pos_ref.shape[0]
    cw = MXU_DIM
    ang = pos_ref[...].astype(F32) * invf_ref[...]
    c, s = jnp.cos(ang), jnp.sin(ang)
    lane = lax.broadcasted_iota(jnp.int32, (tm, LANES), 1)
    low = lane < 64
    c_sw, s_sw = pltpu.roll(c, 64, 1), pltpu.roll(s, 64, 1)
    l64 = lane % 64
    ret_first = l64 < RET_QK_DIM // 2
    da_first = l64 < DA_ROT_DIM // 2
    c_r = jnp.where(low, c, c_sw)
    s_r = jnp.where(low, s, s_sw)
    s_r = jnp.where(ret_first, -s_r, s_r)
    c_d = jnp.where(low, c_sw, c)
    s_d = jnp.where(low, s_sw, s)
    s_d = jnp.where(da_first, -s_d, s_d)

    def wide(t):
        return jnp.concatenate([t, t], axis=1)

    c_r, s_r, c_d, s_d = wide(c_r), wide(s_r), wide(c_d), wide(s_d)
    ret_first, da_first = wide(ret_first), wide(da_first)

    ones_bd = ones_ref[...]
    for ch in range(dqk_ref.shape[1] // cw):
        sl = slice(ch * cw, (ch + 1) * cw)
        x = dqk_ref[:, sl].astype(F32)
        ss = jnp.dot((x * x).astype(BF16), ones_bd, preferred_element_type=F32)
        xn = x * lax.rsqrt(ss * (1.0 / DA_QK_DIM) + EPS) * gain_ref[:, sl]
        y = _rotate(xn, c_d, s_d, da_first, DA_ROT_DIM // 2)
        if ch * cw < DA_K_OFF:
            y = y * (DA_QK_DIM ** -0.5)
        dqk_out[:, sl] = y.astype(BF16)
    for ch in range(rqk_ref.shape[1] // cw):
        sl = slice(ch * cw, (ch + 1) * cw)
        x = rqk_ref[:, sl].astype(F32)
        y = _rotate(x, c_r, s_r, ret_first, RET_QK_DIM // 2)
        if ch * cw >= RET_K_OFF - RET_Q_OFF:
            y = y * (RET_QK_DIM ** -0.5)
        rqk_out[:, sl] = y.astype(BF16)


def _qk_prep(proj, positions, da_q_g, da_k_g):
    s = proj.shape[0]
    tm = PREP_TM
    half_r = RET_QK_DIM // 2
    half_d = DA_ROT_DIM // 2
    inv_r = RET_ROT_BASE ** (-jnp.arange(half_r, dtype=F32) * 2.0 / RET_QK_DIM)
    inv_d = ROPE_THETA ** (-jnp.arange(half_d, dtype=F32) * 2.0 / DA_ROT_DIM)
    inv = jnp.concatenate([inv_r, inv_r, inv_d, inv_d, jnp.zeros((64 - DA_ROT_DIM,), F32)]).reshape(1, LANES)
    gain = jnp.concatenate([jnp.tile(da_q_g.astype(F32), 2 * DA_HEADS),
                            jnp.tile(da_k_g.astype(F32), 2 * DA_HEADS)]).reshape(1, 2048)
    blk = jnp.arange(MXU_DIM) // DA_QK_DIM
    ones_bd = (blk[:, None] == blk[None, :]).astype(BF16)
    return pl.pallas_call(
        _qk_prep_body,
        out_shape=(jax.ShapeDtypeStruct((s, 2048), BF16), jax.ShapeDtypeStruct((s, 1024), BF16)),
        grid=(s // tm,),
        in_specs=[
            pl.BlockSpec((tm, 1), lambda i: (i, 0)),
            pl.BlockSpec((1, LANES), lambda i: (0, 0)),
            pl.BlockSpec((tm, 2048), lambda i: (i, DA_Q_OFF // 2048)),
            pl.BlockSpec((tm, 1024), lambda i: (i, RET_Q_OFF // 1024)),
            pl.BlockSpec((1, 2048), lambda i: (0, 0)),
            pl.BlockSpec((MXU_DIM, MXU_DIM), lambda i: (0, 0)),
        ],
        out_specs=(pl.BlockSpec((tm, 2048), lambda i: (i, 0)),
                   pl.BlockSpec((tm, 1024), lambda i: (i, 0))),
        compiler_params=_params("arbitrary"),
        name="qk_prep",
    )(positions.reshape(s, 1), inv, proj, proj, gain, ones_bd)


def _diff_attn_body(lam_ref, g_ref, q_ref, k_ref, v_ref, o_ref, m_ref, l_ref, acc_ref, *, lambda_init):
    tq = q_ref.shape[0]
    tk = tq
    qi = pl.program_id(1)
    lane = lax.broadcasted_iota(jnp.int32, (tq, LANES), 1)
    qf = q_ref[...].astype(F32)
    qs = (jnp.where(lane < DA_QK_DIM, qf, 0.0).astype(BF16),
          jnp.where(lane >= DA_QK_DIM, qf, 0.0).astype(BF16))
    m_ref[...] = jnp.full(m_ref.shape, NEG_BIG, F32)
    l_ref[...] = jnp.zeros(l_ref.shape, F32)
    acc_ref[...] = jnp.zeros(acc_ref.shape, F32)

    def block(j, masked):
        off = pl.multiple_of(j * tk, tk)
        ks = k_ref[pl.ds(off, tk), :]
        vs = v_ref[pl.ds(off, tk), :]
        if masked:
            keep = (lax.broadcasted_iota(jnp.int32, (tq, tk), 0)
                    >= lax.broadcasted_iota(jnp.int32, (tq, tk), 1))
        for mp in range(2):
            sc = lax.dot_general(qs[mp], ks, (((1,), (1,)), ((), ())), preferred_element_type=F32)
            if masked:
                sc = jnp.where(keep, sc, NEG_BIG)
            m_old = m_ref[mp]
            m_new = jnp.maximum(m_old, jnp.max(sc, axis=-1, keepdims=True))
            alpha = jnp.exp(m_old - m_new)
            p = jnp.exp(sc - m_new)
            l_ref[mp] = alpha * l_ref[mp] + jnp.sum(p, axis=-1, keepdims=True)
            acc_ref[mp] = alpha * acc_ref[mp] + jnp.dot(p.astype(BF16), vs, preferred_element_type=F32)
            m_ref[mp] = m_new

    def loop_body(j, carry):
        block(j, False)
        return carry

    lax.fori_loop(0, qi, loop_body, 0)
    block(qi, True)

    lq1, lk1 = lam_ref[0:1, :], lam_ref[1:2, :]
    lq2, lk2 = lam_ref[2:3, :], lam_ref[3:4, :]
    lam = (jnp.exp(jnp.sum(lq1 * lk1, axis=-1, keepdims=True))
           - jnp.exp(jnp.sum(lq2 * lk2, axis=-1, keepdims=True)) + lambda_init)
    o = acc_ref[0] / l_ref[0] - lam * (acc_ref[1] / l_ref[1])
    o_ref[...] = (_rms(o, g_ref[...]) * (1.0 - lambda_init)).astype(o_ref.dtype)


def _diff_attn(dqk, proj, lam_params, subln_g, lambda_init):
    s = dqk.shape[0]
    tq = DA_TQ
    body = functools.partial(_diff_attn_body, lambda_init=lambda_init)
    return pl.pallas_call(
        body,
        out_shape=jax.ShapeDtypeStruct((s, DA_HEADS * DA_V_DIM), BF16),
        grid=(DA_HEADS, s // tq),
        in_specs=[
            pl.BlockSpec((4, DA_QK_DIM), lambda h, i: (0, 0)),
            pl.BlockSpec((1, DA_V_DIM), lambda h, i: (0, 0)),
            pl.BlockSpec((tq, LANES), lambda h, i: (i, h)),
            pl.BlockSpec((s, LANES), lambda h, i: (0, DA_K_OFF // LANES + h)),
            pl.BlockSpec((s, LANES), lambda h, i: (0, DA_V_OFF // LANES + h)),
        ],
        out_specs=pl.BlockSpec((tq, DA_V_DIM), lambda h, i: (i, h)),
        scratch_shapes=[pltpu.VMEM((2, tq, 1), F32), pltpu.VMEM((2, tq, 1), F32),
                        pltpu.VMEM((2, tq, DA_V_DIM), F32)],
        compiler_params=_params("arbitrary", "arbitrary"),
        name="diff_attn",
    )(lam_params, subln_g, dqk, dqk, proj)


def _retention_body(q_ref, k_ref, v_ref, gate_ref, g_ref, o_ref, state_ref, decay_ref, qdec_ref, kdec_ref):
    c = q_ref.shape[0]
    log_gamma = [math.log(1.0 - 2.0 ** (-5.0 - h)) for h in range(RET_HEADS)]

    @pl.when(pl.program_id(0) == 0)
    def _():
        state_ref[...] = jnp.zeros(state_ref.shape, F32)
        row = lax.broadcasted_iota(jnp.int32, (c, c), 0)
        col = lax.broadcasted_iota(jnp.int32, (c, c), 1)
        rel = (row - col).astype(F32)
        idx = lax.broadcasted_iota(jnp.int32, (c, 1), 0).astype(F32)
        for h in range(RET_HEADS):
            decay_ref[h] = jnp.where(rel >= 0, jnp.exp(log_gamma[h] * jnp.maximum(rel, 0.0)), 0.0)
            qdec_ref[h] = jnp.exp(log_gamma[h] * (idx + 1.0))
            kdec_ref[h] = jnp.exp(log_gamma[h] * (c - 1.0 - idx))

    lane = lax.broadcasted_iota(jnp.int32, (c, LANES), 1)
    for h in range(RET_HEADS):
        pair = slice((h // 2) * LANES, (h // 2 + 1) * LANES)
        mine = (lane >= RET_QK_DIM) if h % 2 else (lane < RET_QK_DIM)
        qh = jnp.where(mine, q_ref[:, pair].astype(F32), 0.0)
        kh = jnp.where(mine, k_ref[:, pair].astype(F32), 0.0)
        vh = v_ref[:, h * RET_V_DIM:(h + 1) * RET_V_DIM]
        qb = qh.astype(BF16)
        scores = lax.dot_general(qb, kh.astype(BF16), (((1,), (1,)), ((), ())), preferred_element_type=F32)
        scores = scores * decay_ref[h]
        o_in = jnp.dot(scores.astype(BF16), vh, preferred_element_type=F32)
        state = state_ref[h]
        o_cross = jnp.dot(qb, state.astype(BF16), preferred_element_type=F32) * qdec_ref[h]
        kd = (kh * kdec_ref[h]).astype(BF16)
        state_ref[h] = state * math.exp(log_gamma[h] * c) + lax.dot_general(
            kd, vh, (((0,), (0,)), ((), ())), preferred_element_type=F32)
        o = _rms(o_in + o_cross, g_ref[...])
        gate = gate_ref[:, h * RET_V_DIM:(h + 1) * RET_V_DIM].astype(F32)
        o_ref[:, h * RET_V_DIM:(h + 1) * RET_V_DIM] = (gate * jax.nn.sigmoid(gate) * o).astype(o_ref.dtype)


def _retention(rqk, proj, norm_g):
    s = rqk.shape[0]
    c = RET_CHUNK
    w = RET_HEADS * RET_V_DIM
    return pl.pallas_call(
        _retention_body,
        out_shape=jax.ShapeDtypeStruct((s, w), BF16),
        grid=(s // c,),
        in_specs=[
            pl.BlockSpec((c, 512), lambda i: (i, 0)),
            pl.BlockSpec((c, 512), lambda i: (i, 1)),
            pl.BlockSpec((c, w), lambda i: (i, RET_V_OFF // w)),
            pl.BlockSpec((c, w), lambda i: (i, RET_G_OFF // w)),
            pl.BlockSpec((1, RET_V_DIM), lambda i: (0, 0)),
        ],
        out_specs=pl.BlockSpec((c, w), lambda i: (i, 0)),
        scratch_shapes=[pltpu.VMEM((RET_HEADS, LANES, RET_V_DIM), F32),
                        pltpu.VMEM((RET_HEADS, c, c), F32),
                        pltpu.VMEM((RET_HEADS, c, 1), F32),
                        pltpu.VMEM((RET_HEADS, c, 1), F32)],
        compiler_params=_params("arbitrary"),
        name="retention",
    )(rqk, rqk, proj, proj, norm_g)


def _mem_kv_body(mem_ref, g_ref, w_ref, kg_ref, o_ref):
    m = _rms(mem_ref[...], g_ref[...]).astype(BF16)
    kv = jnp.dot(m, w_ref[...], preferred_element_type=F32)
    kn = _rms(kv, kg_ref[...]) * (MEM_HEAD_DIM ** -0.5)
    o_ref[...] = jnp.where(pl.program_id(0) < MEM_HEADS, kn, kv).astype(o_ref.dtype)


def _mem_kv(mem, mem_norm_g, w_mem_kv, k_g):
    n = mem.shape[0]
    hd = MEM_HEAD_DIM
    return pl.pallas_call(
        _mem_kv_body,
        out_shape=jax.ShapeDtypeStruct((n, 2 * MEM_HEADS * hd), BF16),
        grid=(2 * MEM_HEADS,),
        in_specs=[
            pl.BlockSpec((n, D_MODEL), lambda j: (0, 0)),
            pl.BlockSpec((1, D_MODEL), lambda j: (0, 0)),
            pl.BlockSpec((D_MODEL, hd), lambda j: (0, j)),
            pl.BlockSpec((1, hd), lambda j: (0, 0)),
        ],
        out_specs=pl.BlockSpec((n, hd), lambda j: (0, j)),
        compiler_params=_params("arbitrary"),
        name="mem_kv",
    )(mem, mem_norm_g, w_mem_kv, k_g)


def _merge_body(x_ref, oda_ref, oret_ref, mq_ref, gda_ref, gret_ref, gmem_ref, kv_ref, qg_ref,
                wda_ref, wret_ref, wmem_ref, wout_ref, o_ref, omem_ref):
    hd = MEM_HEAD_DIM

    @pl.when(pl.program_id(1) == 0)
    def _():
        o_ref[...] = x_ref[...]
        for h in range(MEM_HEADS):
            q = _rms(mq_ref[:, h * hd:(h + 1) * hd].astype(F32), qg_ref[...]).astype(BF16)
            k = kv_ref[:, h * hd:(h + 1) * hd]
            v = kv_ref[:, (MEM_HEADS + h) * hd:(MEM_HEADS + h + 1) * hd]
            sc = lax.dot_general(q, k, (((1,), (1,)), ((), ())), preferred_element_type=F32)
            p = jnp.exp(sc - jnp.max(sc, axis=-1, keepdims=True))
            o = jnp.dot(p.astype(BF16), v, preferred_element_type=F32) / jnp.sum(p, axis=-1, keepdims=True)
            omem_ref[:, h * hd:(h + 1) * hd] = o.astype(BF16)

    def branch(gate_ref, act, w_ref):
        return jax.nn.sigmoid(gate_ref[...].astype(F32)) * jnp.dot(act, w_ref[...], preferred_element_type=F32)

    merged = (branch(gda_ref, oda_ref[...], wda_ref) + branch(gret_ref, oret_ref[...], wret_ref)
              + branch(gmem_ref, omem_ref[...], wmem_ref))
    o_ref[...] += jnp.dot(merged.astype(BF16), wout_ref[...], preferred_element_type=F32)


def _merge(x, o_da, o_ret, proj, kv_mem, mem_q_g, w_o_da, w_o_ret, w_o_mem, w_out):
    s = x.shape[0]
    tm, tn = MERGE_TM, MERGE_TN
    n_mem = kv_mem.shape[0]
    bw = 1024
    gate_blk = GATE_OFF // tn
    per_gate = D_MODEL // tn
    row = lambda i, j: (i, 0)
    return pl.pallas_call(
        _merge_body,
        out_shape=jax.ShapeDtypeStruct((s, D_MODEL), F32),
        grid=(s // tm, D_MODEL // tn),
        in_specs=[
            pl.BlockSpec((tm, D_MODEL), row),
            pl.BlockSpec((tm, bw), row),
            pl.BlockSpec((tm, bw), row),
            pl.BlockSpec((tm, bw), lambda i, j: (i, MEM_Q_OFF // bw)),
            pl.BlockSpec((tm, tn), lambda i, j: (i, gate_blk + j)),
            pl.BlockSpec((tm, tn), lambda i, j: (i, gate_blk + per_gate + j)),
            pl.BlockSpec((tm, tn), lambda i, j: (i, gate_blk + 2 * per_gate + j)),
            pl.BlockSpec((n_mem, 2 * bw), lambda i, j: (0, 0)),
            pl.BlockSpec((1, MEM_HEAD_DIM), lambda i, j: (0, 0)),
            pl.BlockSpec((bw, tn), lambda i, j: (0, j)),
            pl.BlockSpec((bw, tn), lambda i, j: (0, j)),
            pl.BlockSpec((bw, tn), lambda i, j: (0, j)),
            pl.BlockSpec((tn, D_MODEL), lambda i, j: (j, 0)),
        ],
        out_specs=pl.BlockSpec((tm, D_MODEL), row),
        scratch_shapes=[pltpu.VMEM((tm, bw), BF16)],
        compiler_params=_params("arbitrary", "arbitrary"),
        name="merge",
    )(x, o_da, o_ret, proj, proj, proj, proj, kv_mem, mem_q_g, w_o_da, w_o_ret, w_o_mem, w_out)


def _ffn_body(x_ref, g_ref, wg_ref, wu_ref, wd_ref, o_ref, h_ref):
    @pl.when(pl.program_id(1) == 0)
    def _():
        x = x_ref[...]
        o_ref[...] = x
        h_ref[...] = _rms(x, g_ref[...]).astype(BF16)

    h = h_ref[...]
    gate = jnp.dot(h, wg_ref[...], preferred_element_type=F32)
    up = jnp.dot(h, wu_ref[...], preferred_element_type=F32)
    act = (gate * jax.nn.sigmoid(gate) * up).astype(BF16)
    o_ref[...] += jnp.dot(act, wd_ref[...], preferred_element_type=F32)


def _ffn(x, g, w_gate, w_up, w_down):
    s = x.shape[0]
    tm, tf = FFN_TM, FFN_TF
    return pl.pallas_call(
        _ffn_body,
        out_shape=jax.ShapeDtypeStruct((s, D_MODEL), F32),
        grid=(s // tm, D_FF // tf),
        in_specs=[
            pl.BlockSpec((tm, D_MODEL), lambda i, j: (i, 0)),
            pl.BlockSpec((1, D_MODEL), lambda i, j: (0, 0)),
            pl.BlockSpec((D_MODEL, tf), lambda i, j: (0, j)),
            pl.BlockSpec((D_MODEL, tf), lambda i, j: (0, j)),
            pl.BlockSpec((tf, D_MODEL), lambda i, j: (j, 0)),
        ],
        out_specs=pl.BlockSpec((tm, D_MODEL), lambda i, j: (i, 0)),
        scratch_shapes=[pltpu.VMEM((tm, D_MODEL), BF16)],
        compiler_params=_params("arbitrary", "arbitrary"),
        name="ffn",
    )(x, g, w_gate, w_up, w_down)


def _layer(x, mem, positions, l, attn_norm_g, w_in, da_q_norm_g, da_k_norm_g, da_lambda_q1, da_lambda_k1,
           da_lambda_q2, da_lambda_k2, da_subln_g, ret_norm_g, mem_norm_g, w_mem_kv, mem_q_norm_g,
           mem_k_norm_g, w_o_da, w_o_ret, w_o_mem, w_out, ffn_norm_g, w_ffn_gate, w_ffn_up, w_ffn_down):
    lambda_init = 0.8 - 0.6 * math.exp(-0.3 * l)
    row = lambda a: a.astype(F32).reshape(1, -1)
    proj = _in_proj(x, row(attn_norm_g), w_in.astype(BF16))
    dqk, rqk = _qk_prep(proj, positions, da_q_norm_g, da_k_norm_g)
    lam_params = jnp.stack([da_lambda_q1, da_lambda_k1, da_lambda_q2, da_lambda_k2]).astype(F32)
    o_da = _diff_attn(dqk, proj, lam_params, row(da_subln_g), lambda_init)
    o_ret = _retention(rqk, proj, row(ret_norm_g))
    kv_mem = _mem_kv(mem, row(mem_norm_g), w_mem_kv.astype(BF16), row(mem_k_norm_g))
    x1 = _merge(x, o_da, o_ret, proj, kv_mem, row(mem_q_norm_g), w_o_da.astype(BF16),
                w_o_ret.astype(BF16), w_o_mem.astype(BF16), w_out.astype(BF16))
    return _ffn(x1, row(ffn_norm_g), w_ffn_gate.astype(BF16), w_ffn_up.astype(BF16),
                w_ffn_down.astype(BF16))


def kernel(x, mem, positions, attn_norm_g, w_in, da_q_norm_g, da_k_norm_g, da_lambda_q1, da_lambda_k1,
           da_lambda_q2, da_lambda_k2, da_subln_g, ret_norm_g, mem_norm_g, w_mem_kv, mem_q_norm_g,
           mem_k_norm_g, w_o_da, w_o_ret, w_o_mem, w_out, ffn_norm_g, w_ffn_gate, w_ffn_up, w_ffn_down):
    batch, depth = x.shape[0], w_in.shape[0]
    outs = []
    for b in range(batch):
        xb = x[b]
        for l in range(depth):
            xb = _layer(xb, mem[b], positions[b], l, attn_norm_g[l], w_in[l], da_q_norm_g[l], da_k_norm_g[l],
                        da_lambda_q1[l], da_lambda_k1[l], da_lambda_q2[l], da_lambda_k2[l], da_subln_g[l],
                        ret_norm_g[l], mem_norm_g[l], w_mem_kv[l], mem_q_norm_g[l], mem_k_norm_g[l],
                        w_o_da[l], w_o_ret[l], w_o_mem[l], w_out[l], ffn_norm_g[l], w_ffn_gate[l],
                        w_ffn_up[l], w_ffn_down[l])
        outs.append(xb)
    return jnp.stack(outs)
```

```python
import functools
import math

import jax
import jax.numpy as jnp
from jax import lax
from jax.experimental import pallas as pl
from jax.experimental.pallas import tpu as pltpu

F32 = jnp.float32
BF16 = jnp.bfloat16

D_MODEL = 2048
EPS = 1e-6
LOG2E = math.log2(math.e)

DA_HEADS = 8
DA_QK_DIM = 64
DA_V_DIM = 128
DA_ROT_DIM = 16
ROPE_THETA = 500000.0

RET_HEADS = 8
RET_QK_DIM = 64
RET_V_DIM = 128
RET_ROT_BASE = 10000.0

MEM_HEADS = 4
MEM_HEAD_DIM = 256

D_FF = 5632

DA_Q_OFF = 0
DA_K_OFF = 1024
DA_V_OFF = 2048
RET_Q_OFF = 3072
RET_K_OFF = 3584
RET_V_OFF = 4096
RET_G_OFF = 5120
MEM_Q_OFF = 6144
GATE_OFF = 7168
IN_COLS = 13312

LANES = 128
BF16_SUBLANES = 16
MXU_DIM = 256
VMEM_LIMIT = 60 * 1024 * 1024
NEG_BIG = -1e30
DA_VT_ROWS = DA_V_DIM + BF16_SUBLANES

IN_TM, IN_TN = 1024, 1024
PREP_TM = 512
DA_TQ = 512
RET_CHUNK = 256
MERGE_TM, MERGE_TN = 512, 512
FFN_TM, FFN_TF = 512, 512


def _params(*sem):
    return pltpu.CompilerParams(dimension_semantics=sem, vmem_limit_bytes=VMEM_LIMIT)


def _rms(x, g):
    return x * lax.rsqrt(jnp.mean(x * x, axis=-1, keepdims=True) + EPS) * g


def _in_proj_body(x_ref, g_ref, w_ref, o_ref, h_ref):
    @pl.when(pl.program_id(1) == 0)
    def _():
        h_ref[...] = _rms(x_ref[...], g_ref[...]).astype(BF16)

    o_ref[...] = jnp.dot(h_ref[...], w_ref[...], preferred_element_type=F32).astype(o_ref.dtype)


def _in_proj(x, g, w):
    s = x.shape[0]
    return pl.pallas_call(
        _in_proj_body,
        out_shape=jax.ShapeDtypeStruct((s, IN_COLS), BF16),
        grid=(s // IN_TM, IN_COLS // IN_TN),
        in_specs=[
            pl.BlockSpec((IN_TM, D_MODEL), lambda i, j: (i, 0)),
            pl.BlockSpec((1, D_MODEL), lambda i, j: (0, 0)),
            pl.BlockSpec((D_MODEL, IN_TN), lambda i, j: (0, j)),
        ],
        out_specs=pl.BlockSpec((IN_TM, IN_TN), lambda i, j: (i, j)),
        scratch_shapes=[pltpu.VMEM((IN_TM, D_MODEL), BF16)],
        compiler_params=_params("arbitrary", "arbitrary"),
        name="in_proj",
    )(x, g, w)


def _rotate(x, cos, sin_signed, first_half, half):
    w = x.shape[-1]
    partner = jnp.where(first_half, pltpu.roll(x, w - half, 1), pltpu.roll(x, half, 1))
    return x * cos + partner * sin_signed


def _qk_prep_body(pos_ref, invf_ref, dqk_ref, v_ref, rqk_ref, gain_ref, ones_ref,
                  qt_out, k_out, vt_out, rqk_out):
    tm = pos_ref.shape[0]
    cw = MXU_DIM
    ang = pos_ref[...].astype(F32) * invf_ref[...]
    c, s = jnp.cos(ang), jnp.sin(ang)
    lane = lax.broadcasted_iota(jnp.int32, (tm, LANES), 1)
    low = lane < 64
    c_sw, s_sw = pltpu.roll(c, 64, 1), pltpu.roll(s, 64, 1)
    l64 = lane % 64
    ret_first = l64 < RET_QK_DIM // 2
    da_first = l64 < DA_ROT_DIM // 2
    c_r = jnp.where(low, c, c_sw)
    s_r = jnp.where(low, s, s_sw)
    s_r = jnp.where(ret_first, -s_r, s_r)
    c_d = jnp.where(low, c_sw, c)
    s_d = jnp.where(low, s_sw, s)
    s_d = jnp.where(da_first, -s_d, s_d)

    def wide(t):
        return jnp.concatenate([t, t], axis=1)

    c_r, s_r, c_d, s_d = wide(c_r), wide(s_r), wide(c_d), wide(s_d)
    ret_first, da_first = wide(ret_first), wide(da_first)

    ones_bd = ones_ref[...]
    n_q = DA_K_OFF // cw
    for ch in range(dqk_ref.shape[1] // cw):
        sl = slice(ch * cw, (ch + 1) * cw)
        x = dqk_ref[:, sl].astype(F32)
        ss = jnp.dot((x * x).astype(BF16), ones_bd, preferred_element_type=F32)
        xn = x * lax.rsqrt(ss * (1.0 / DA_QK_DIM) + EPS) * gain_ref[:, sl]
        y = _rotate(xn, c_d, s_d, da_first, DA_ROT_DIM // 2)
        if ch < n_q:
            qt_out[sl, :] = (y * (DA_QK_DIM ** -0.5 * LOG2E)).T.astype(BF16)
        else:
            k_out[:, (ch - n_q) * cw:(ch - n_q + 1) * cw] = y.astype(BF16)
    for h in range(DA_HEADS):
        vt_out[h, 0:DA_V_DIM, :] = v_ref[:, h * DA_V_DIM:(h + 1) * DA_V_DIM].astype(F32).T.astype(BF16)
        vt_out[h, DA_V_DIM:DA_VT_ROWS, :] = jnp.ones((DA_VT_ROWS - DA_V_DIM, tm), BF16)
    for ch in range(rqk_ref.shape[1] // cw):
        sl = slice(ch * cw, (ch + 1) * cw)
        x = rqk_ref[:, sl].astype(F32)
        y = _rotate(x, c_r, s_r, ret_first, RET_QK_DIM // 2)
        if ch * cw >= RET_K_OFF - RET_Q_OFF:
            y = y * (RET_QK_DIM ** -0.5)
        rqk_out[:, sl] = y.astype(BF16)


def _qk_prep(proj, positions, da_q_g, da_k_g):
    s = proj.shape[0]
    tm = PREP_TM
    half_r = RET_QK_DIM // 2
    half_d = DA_ROT_DIM // 2
    inv_r = RET_ROT_BASE ** (-jnp.arange(half_r, dtype=F32) * 2.0 / RET_QK_DIM)
    inv_d = ROPE_THETA ** (-jnp.arange(half_d, dtype=F32) * 2.0 / DA_ROT_DIM)
    inv = jnp.concatenate([inv_r, inv_r, inv_d, inv_d, jnp.zeros((64 - DA_ROT_DIM,), F32)]).reshape(1, LANES)
    gain = jnp.concatenate([jnp.tile(da_q_g.astype(F32), 2 * DA_HEADS),
                            jnp.tile(da_k_g.astype(F32), 2 * DA_HEADS)]).reshape(1, 2048)
    blk = jnp.arange(MXU_DIM) // DA_QK_DIM
    ones_bd = (blk[:, None] == blk[None, :]).astype(BF16)
    return pl.pallas_call(
        _qk_prep_body,
        out_shape=(jax.ShapeDtypeStruct((1024, s), BF16),
                   jax.ShapeDtypeStruct((s, 1024), BF16),
                   jax.ShapeDtypeStruct((DA_HEADS, DA_VT_ROWS, s), BF16),
                   jax.ShapeDtypeStruct((s, 1024), BF16)),
        grid=(s // tm,),
        in_specs=[
            pl.BlockSpec((tm, 1), lambda i: (i, 0)),
            pl.BlockSpec((1, LANES), lambda i: (0, 0)),
            pl.BlockSpec((tm, 2048), lambda i: (i, DA_Q_OFF // 2048)),
            pl.BlockSpec((tm, 1024), lambda i: (i, DA_V_OFF // 1024)),
            pl.BlockSpec((tm, 1024), lambda i: (i, RET_Q_OFF // 1024)),
            pl.BlockSpec((1, 2048), lambda i: (0, 0)),
            pl.BlockSpec((MXU_DIM, MXU_DIM), lambda i: (0, 0)),
        ],
        out_specs=(pl.BlockSpec((1024, tm), lambda i: (0, i)),
                   pl.BlockSpec((tm, 1024), lambda i: (i, 0)),
                   pl.BlockSpec((DA_HEADS, DA_VT_ROWS, tm), lambda i: (0, 0, i)),
                   pl.BlockSpec((tm, 1024), lambda i: (i, 0))),
        compiler_params=_params("arbitrary"),
        name="qk_prep",
    )(positions.reshape(s, 1), inv, proj, proj, proj, gain, ones_bd)


def _diff_attn_body(lam_ref, g_ref, qt_ref, k_ref, vt_ref, o_ref, sa_ref, sb_ref, mxa_ref, mxb_ref,
                    m_ref, acc_ref, *, lambda_init):
    tq = qt_ref.shape[1]
    tk = tq
    qi = pl.program_id(1)
    last = k_ref.shape[0] // tk - 1
    row = lax.broadcasted_iota(jnp.int32, (LANES, tq), 0)
    qf = qt_ref[...].astype(F32)
    qz = (jnp.where(row < DA_QK_DIM, qf, 0.0).astype(BF16),
          jnp.where(row >= DA_QK_DIM, qf, 0.0).astype(BF16))
    m_ref[...] = jnp.full(m_ref.shape, NEG_BIG, F32)
    acc_ref[...] = jnp.zeros(acc_ref.shape, F32)
    rel = (lax.broadcasted_iota(jnp.int32, (tk, tq), 0) - lax.broadcasted_iota(jnp.int32, (tk, tq), 1))

    def scores(j, s_ref, mx_ref, masked):
        off = pl.multiple_of(jnp.minimum(j, last) * tk, tk)
        ks = k_ref[pl.ds(off, tk), :]
        for mp in range(2):
            sc = jnp.dot(ks, qz[mp], preferred_element_type=F32)
            if masked:
                sc = jnp.where(rel <= (qi - j) * tq, sc, NEG_BIG)
            s_ref[mp] = sc
            mx_ref[mp] = jnp.max(sc, axis=0, keepdims=True)

    def update(j, s_ref, mx_ref):
        off = pl.multiple_of(jnp.minimum(j, last) * tk, tk)
        vts = vt_ref[:, pl.ds(off, tk)]
        for mp in range(2):
            m_old = m_ref[mp]
            m_new = jnp.maximum(m_old, mx_ref[mp])
            alpha = jnp.exp2(m_old - m_new)
            p = jnp.exp2(s_ref[mp] - m_new).astype(BF16)
            acc_ref[mp] = alpha * acc_ref[mp] + jnp.dot(vts, p, preferred_element_type=F32)
            m_ref[mp] = m_new

    scores(0, sa_ref, mxa_ref, True)
    n_pairs = jnp.maximum(qi - 1, 0) // 2

    def pair(t, carry):
        scores(2 * t + 1, sb_ref, mxb_ref, False)
        update(2 * t, sa_ref, mxa_ref)
        scores(2 * t + 2, sa_ref, mxa_ref, False)
        update(2 * t + 1, sb_ref, mxb_ref)
        return carry

    lax.fori_loop(0, n_pairs, pair, 0)
    e0 = 2 * n_pairs
    scores(e0 + 1, sb_ref, mxb_ref, True)
    update(e0, sa_ref, mxa_ref)
    scores(e0 + 2, sa_ref, mxa_ref, True)
    update(e0 + 1, sb_ref, mxb_ref)
    update(e0 + 2, sa_ref, mxa_ref)

    lq1, lk1 = lam_ref[0:1, :], lam_ref[1:2, :]
    lq2, lk2 = lam_ref[2:3, :], lam_ref[3:4, :]
    lam = (jnp.exp(jnp.sum(lq1 * lk1, axis=-1, keepdims=True))
           - jnp.exp(jnp.sum(lq2 * lk2, axis=-1, keepdims=True)) + lambda_init)
    a1, a2 = acc_ref[0], acc_ref[1]
    dv = DA_V_DIM
    ot = a1[0:dv] / a1[dv:dv + 1] - lam * (a2[0:dv] / a2[dv:dv + 1])
    ot = ot * lax.rsqrt(jnp.mean(ot * ot, axis=0, keepdims=True) + EPS)
    o_ref[...] = (ot.T * g_ref[...] * (1.0 - lambda_init)).astype(o_ref.dtype)


def _diff_attn(qt, k, vt, lam_params, subln_g, lambda_init):
    s = k.shape[0]
    tq = DA_TQ
    body = functools.partial(_diff_attn_body, lambda_init=lambda_init)
    return pl.pallas_call(
        body,
        out_shape=jax.ShapeDtypeStruct((s, DA_HEADS * DA_V_DIM), BF16),
        grid=(DA_HEADS, s // tq),
        in_specs=[
            pl.BlockSpec((4, DA_QK_DIM), lambda h, i: (0, 0)),
            pl.BlockSpec((1, DA_V_DIM), lambda h, i: (0, 0)),
            pl.BlockSpec((LANES, tq), lambda h, i: (h, i)),
            pl.BlockSpec((s, LANES), lambda h, i: (0, h)),
            pl.BlockSpec((None, DA_VT_ROWS, s), lambda h, i: (h, 0, 0)),
        ],
        out_specs=pl.BlockSpec((tq, DA_V_DIM), lambda h, i: (i, h)),
        scratch_shapes=[pltpu.VMEM((2, tq, tq), F32), pltpu.VMEM((2, tq, tq), F32),
                        pltpu.VMEM((2, 1, tq), F32), pltpu.VMEM((2, 1, tq), F32),
                        pltpu.VMEM((2, 1, tq), F32), pltpu.VMEM((2, DA_VT_ROWS, tq), F32)],
        compiler_params=_params("arbitrary", "arbitrary"),
        name="diff_attn",
    )(lam_params, subln_g, qt, k, vt)


def _retention_body(q_ref, k_ref, v_ref, gate_ref, g_ref, o_ref, state_ref, decay_ref, qdec_ref, kdec_ref):
    c = q_ref.shape[0]
    log_gamma = [math.log(1.0 - 2.0 ** (-5.0 - h)) for h in range(RET_HEADS)]

    @pl.when(pl.program_id(0) == 0)
    def _():
        state_ref[...] = jnp.zeros(state_ref.shape, F32)
        row = lax.broadcasted_iota(jnp.int32, (c, c), 0)
        col = lax.broadcasted_iota(jnp.int32, (c, c), 1)
        rel = (row - col).astype(F32)
        idx = lax.broadcasted_iota(jnp.int32, (c, 1), 0).astype(F32)
        for h in range(RET_HEADS):
            decay_ref[h] = jnp.where(rel >= 0, jnp.exp(log_gamma[h] * jnp.maximum(rel, 0.0)), 0.0)
            qdec_ref[h] = jnp.exp(log_gamma[h] * (idx + 1.0))
            kdec_ref[h] = jnp.exp(log_gamma[h] * (c - 1.0 - idx))

    lane = lax.broadcasted_iota(jnp.int32, (c, LANES), 1)
    for h in range(RET_HEADS):
        pair = slice((h // 2) * LANES, (h // 2 + 1) * LANES)
        mine = (lane >= RET_QK_DIM) if h % 2 else (lane < RET_QK_DIM)
        qh = jnp.where(mine, q_ref[:, pair].astype(F32), 0.0)
        kh = jnp.where(mine, k_ref[:, pair].astype(F32), 0.0)
        vh = v_ref[:, h * RET_V_DIM:(h + 1) * RET_V_DIM]
        qb = qh.astype(BF16)
        scores = lax.dot_general(qb, kh.astype(BF16), (((1,), (1,)), ((), ())), preferred_element_type=F32)
        scores = scores * decay_ref[h]
        o_in = jnp.dot(scores.astype(BF16), vh, preferred_element_type=F32)
        state = state_ref[h]
        o_cross = jnp.dot(qb, state.astype(BF16), preferred_element_type=F32) * qdec_ref[h]
        kd = (kh * kdec_ref[h]).astype(BF16)
        state_ref[h] = state * math.exp(log_gamma[h] * c) + lax.dot_general(
            kd, vh, (((0,), (0,)), ((), ())), preferred_element_type=F32)
        o = _rms(o_in + o_cross, g_ref[...])
        gate = gate_ref[:, h * RET_V_DIM:(h + 1) * RET_V_DIM].astype(F32)
        o_ref[:, h * RET_V_DIM:(h + 1) * RET_V_DIM] = (gate * jax.nn.sigmoid(gate) * o).astype(o_ref.dtype)


def _retention(rqk, proj, norm_g):
    s = rqk.shape[0]
    c = RET_CHUNK
    w = RET_HEADS * RET_V_DIM
    return pl.pallas_call(
        _retention_body,
        out_shape=jax.ShapeDtypeStruct((s, w), BF16),
        grid=(s // c,),
        in_specs=[
            pl.BlockSpec((c, 512), lambda i: (i, 0)),
            pl.BlockSpec((c, 512), lambda i: (i, 1)),
            pl.BlockSpec((c, w), lambda i: (i, RET_V_OFF // w)),
            pl.BlockSpec((c, w), lambda i: (i, RET_G_OFF // w)),
            pl.BlockSpec((1, RET_V_DIM), lambda i: (0, 0)),
        ],
        out_specs=pl.BlockSpec((c, w), lambda i: (i, 0)),
        scratch_shapes=[pltpu.VMEM((RET_HEADS, LANES, RET_V_DIM), F32),
                        pltpu.VMEM((RET_HEADS, c, c), F32),
                        pltpu.VMEM((RET_HEADS, c, 1), F32),
                        pltpu.VMEM((RET_HEADS, c, 1), F32)],
        compiler_params=_params("arbitrary"),
        name="retention",
    )(rqk, rqk, proj, proj, norm_g)


def _mem_kv_body(mem_ref, g_ref, w_ref, kg_ref, o_ref):
    m = _rms(mem_ref[...], g_ref[...]).astype(BF16)
    kv = jnp.dot(m, w_ref[...], preferred_element_type=F32)
    kn = _rms(kv, kg_ref[...]) * (MEM_HEAD_DIM ** -0.5)
    o_ref[...] = jnp.where(pl.program_id(0) < MEM_HEADS, kn, kv).astype(o_ref.dtype)


def _mem_kv(mem, mem_norm_g, w_mem_kv, k_g):
    n = mem.shape[0]
    hd = MEM_HEAD_DIM
    return pl.pallas_call(
        _mem_kv_body,
        out_shape=jax.ShapeDtypeStruct((n, 2 * MEM_HEADS * hd), BF16),
        grid=(2 * MEM_HEADS,),
        in_specs=[
            pl.BlockSpec((n, D_MODEL), lambda j: (0, 0)),
            pl.BlockSpec((1, D_MODEL), lambda j: (0, 0)),
            pl.BlockSpec((D_MODEL, hd), lambda j: (0, j)),
            pl.BlockSpec((1, hd), lambda j: (0, 0)),
        ],
        out_specs=pl.BlockSpec((n, hd), lambda j: (0, j)),
        compiler_params=_params("arbitrary"),
        name="mem_kv",
    )(mem, mem_norm_g, w_mem_kv, k_g)


def _merge_body(x_ref, oda_ref, oret_ref, mq_ref, gda_ref, gret_ref, gmem_ref, kv_ref, qg_ref,
                wda_ref, wret_ref, wmem_ref, wout_ref, o_ref, omem_ref):
    hd = MEM_HEAD_DIM

    @pl.when(pl.program_id(1) == 0)
    def _():
        o_ref[...] = x_ref[...]
        for h in range(MEM_HEADS):
            q = _rms(mq_ref[:, h * hd:(h + 1) * hd].astype(F32), qg_ref[...]).astype(BF16)
            k = kv_ref[:, h * hd:(h + 1) * hd]
            v = kv_ref[:, (MEM_HEADS + h) * hd:(MEM_HEADS + h + 1) * hd]
            sc = lax.dot_general(q, k, (((1,), (1,)), ((), ())), preferred_element_type=F32)
            p = jnp.exp(sc - jnp.max(sc, axis=-1, keepdims=True))
            o = jnp.dot(p.astype(BF16), v, preferred_element_type=F32) / jnp.sum(p, axis=-1, keepdims=True)
            omem_ref[:, h * hd:(h + 1) * hd] = o.astype(BF16)

    def branch(gate_ref, act, w_ref):
        return jax.nn.sigmoid(gate_ref[...].astype(F32)) * jnp.dot(act, w_ref[...], preferred_element_type=F32)

    merged = (branch(gda_ref, oda_ref[...], wda_ref) + branch(gret_ref, oret_ref[...], wret_ref)
              + branch(gmem_ref, omem_ref[...], wmem_ref))
    o_ref[...] += jnp.dot(merged.astype(BF16), wout_ref[...], preferred_element_type=F32)


def _merge(x, o_da, o_ret, proj, kv_mem, mem_q_g, w_o_da, w_o_ret, w_o_mem, w_out):
    s = x.shape[0]
    tm, tn = MERGE_TM, MERGE_TN
    n_mem = kv_mem.shape[0]
    bw = 1024
    gate_blk = GATE_OFF // tn
    per_gate = D_MODEL // tn
    row = lambda i, j: (i, 0)
    return pl.pallas_call(
        _merge_body,
        out_shape=jax.ShapeDtypeStruct((s, D_MODEL), F32),
        grid=(s // tm, D_MODEL // tn),
        in_specs=[
            pl.BlockSpec((tm, D_MODEL), row),
            pl.BlockSpec((tm, bw), row),
            pl.BlockSpec((tm, bw), row),
            pl.BlockSpec((tm, bw), lambda i, j: (i, MEM_Q_OFF // bw)),
            pl.BlockSpec((tm, tn), lambda i, j: (i, gate_blk + j)),
            pl.BlockSpec((tm, tn), lambda i, j: (i, gate_blk + per_gate + j)),
            pl.BlockSpec((tm, tn), lambda i, j: (i, gate_blk + 2 * per_gate + j)),
            pl.BlockSpec((n_mem, 2 * bw), lambda i, j: (0, 0)),
            pl.BlockSpec((1, MEM_HEAD_DIM), lambda i, j: (0, 0)),
            pl.BlockSpec((bw, tn), lambda i, j: (0, j)),
            pl.BlockSpec((bw, tn), lambda i, j: (0, j)),
            pl.BlockSpec((bw, tn), lambda i, j: (0, j)),
            pl.BlockSpec((tn, D_MODEL), lambda i, j: (j, 0)),
        ],
        out_specs=pl.BlockSpec((tm, D_MODEL), row),
        scratch_shapes=[pltpu.VMEM((tm, bw), BF16)],
        compiler_params=_params("arbitrary", "arbitrary"),
        name="merge",
    )(x, o_da, o_ret, proj, proj, proj, proj, kv_mem, mem_q_g, w_o_da, w_o_ret, w_o_mem, w_out)


def _ffn_body(x_ref, g_ref, wg_ref, wu_ref, wd_ref, o_ref, h_ref):
    @pl.when(pl.program_id(1) == 0)
    def _():
        x = x_ref[...]
        o_ref[...] = x
        h_ref[...] = _rms(x, g_ref[...]).astype(BF16)

    h = h_ref[...]
    gate = jnp.dot(h, wg_ref[...], preferred_element_type=F32)
    up = jnp.dot(h, wu_ref[...], preferred_element_type=F32)
    act = (gate * jax.nn.sigmoid(gate) * up).astype(BF16)
    o_ref[...] += jnp.dot(act, wd_ref[...], preferred_element_type=F32)


def _ffn(x, g, w_gate, w_up, w_down):
    s = x.shape[0]
    tm, tf = FFN_TM, FFN_TF
    return pl.pallas_call(
        _ffn_body,
        out_shape=jax.ShapeDtypeStruct((s, D_MODEL), F32),
        grid=(s // tm, D_FF // tf),
        in_specs=[
            pl.BlockSpec((tm, D_MODEL), lambda i, j: (i, 0)),
            pl.BlockSpec((1, D_MODEL), lambda i, j: (0, 0)),
            pl.BlockSpec((D_MODEL, tf), lambda i, j: (0, j)),
            pl.BlockSpec((D_MODEL, tf), lambda i, j: (0, j)),
            pl.BlockSpec((tf, D_MODEL), lambda i, j: (j, 0)),
        ],
        out_specs=pl.BlockSpec((tm, D_MODEL), lambda i, j: (i, 0)),
        scratch_shapes=[pltpu.VMEM((tm, D_MODEL), BF16)],
        compiler_params=_params("arbitrary", "arbitrary"),
        name="ffn",
    )(x, g, w_gate, w_up, w_down)


def _layer(x, mem, positions, l, attn_norm_g, w_in, da_q_norm_g, da_k_norm_g, da_lambda_q1, da_lambda_k1,
           da_lambda_q2, da_lambda_k2, da_subln_g, ret_norm_g, mem_norm_g, w_mem_kv, mem_q_norm_g,
           mem_k_norm_g, w_o_da, w_o_ret, w_o_mem, w_out, ffn_norm_g, w_ffn_gate, w_ffn_up, w_ffn_down):
    lambda_init = 0.8 - 0.6 * math.exp(-0.3 * l)
    row = lambda a: a.astype(F32).reshape(1, -1)
    proj = _in_proj(x, row(attn_norm_g), w_in.astype(BF16))
    da_qt, da_k, da_vt, rqk = _qk_prep(proj, positions, da_q_norm_g, da_k_norm_g)
    lam_params = jnp.stack([da_lambda_q1, da_lambda_k1, da_lambda_q2, da_lambda_k2]).astype(F32)
    o_da = _diff_attn(da_qt, da_k, da_vt, lam_params, row(da_subln_g), lambda_init)
    o_ret = _retention(rqk, proj, row(ret_norm_g))
    kv_mem = _mem_kv(mem, row(mem_norm_g), w_mem_kv.astype(BF16), row(mem_k_norm_g))
    x1 = _merge(x, o_da, o_ret, proj, kv_mem, row(mem_q_norm_g), w_o_da.astype(BF16),
                w_o_ret.astype(BF16), w_o_mem.astype(BF16), w_out.astype(BF16))
    return _ffn(x1, row(ffn_norm_g), w_ffn_gate.astype(BF16), w_ffn_up.astype(BF16),
                w_ffn_down.astype(BF16))


def kernel(x, mem, positions, attn_norm_g, w_in, da_q_norm_g, da_k_norm_g, da_lambda_q1, da_lambda_k1,
           da_lambda_q2, da_lambda_k2, da_subln_g, ret_norm_g, mem_norm_g, w_mem_kv, mem_q_norm_g,
           mem_k_norm_g, w_o_da, w_o_ret, w_o_mem, w_out, ffn_norm_g, w_ffn_gate, w_ffn_up, w_ffn_down):
    batch, depth = x.shape[0], w_in.shape[0]
    outs = []
    for b in range(batch):
        xb = x[b]
        for l in range(depth):
            xb = _layer(xb, mem[b], positions[b], l, attn_norm_g[l], w_in[l], da_q_norm_g[l], da_k_norm_g[l],
                        da_lambda_q1[l], da_lambda_k1[l], da_lambda_q2[l], da_lambda_k2[l], da_subln_g[l],
                        ret_norm_g[l], mem_norm_g[l], w_mem_kv[l], mem_q_norm_g[l], mem_k_norm_g[l],
                        w_o_da[l], w_o_ret[l], w_o_mem[l], w_out[l], ffn_norm_g[l], w_ffn_gate[l],
                        w_ffn_up[l], w_ffn_down[l])
        outs.append(xb)
    return jnp.stack(outs)
```

```python
import functools
import math

import jax
import jax.numpy as jnp
from jax import lax
from jax.experimental import pallas as pl
from jax.experimental.pallas import tpu as pltpu

F32 = jnp.float32
BF16 = jnp.bfloat16

D_MODEL = 2048
EPS = 1e-6
LOG2E = math.log2(math.e)

DA_HEADS = 8
DA_QK_DIM = 64
DA_V_DIM = 128
DA_ROT_DIM = 16
ROPE_THETA = 500000.0

RET_HEADS = 8
RET_QK_DIM = 64
RET_V_DIM = 128
RET_ROT_BASE = 10000.0

MEM_HEADS = 4
MEM_HEAD_DIM = 256

D_FF = 5632

DA_Q_OFF = 0
DA_K_OFF = 1024
DA_V_OFF = 2048
RET_Q_OFF = 3072
RET_K_OFF = 3584
RET_V_OFF = 4096
RET_G_OFF = 5120
MEM_Q_OFF = 6144
GATE_OFF = 7168
IN_COLS = 13312

LANES = 128
BF16_SUBLANES = 16
MXU_DIM = 256
VMEM_LIMIT = 60 * 1024 * 1024
NEG_BIG = -1e30
DA_VT_ROWS = DA_V_DIM + BF16_SUBLANES

IN_TM, IN_TN = 1024, 1024
PREP_TM = 512
DA_TQ, DA_TK = 512, 512
RET_CHUNK = 256
MERGE_TM, MERGE_TN = 512, 512
FFN_TM, FFN_TF = 1024, 512


def _params(*sem):
    return pltpu.CompilerParams(dimension_semantics=sem, vmem_limit_bytes=VMEM_LIMIT)


def _rms(x, g):
    return x * lax.rsqrt(jnp.mean(x * x, axis=-1, keepdims=True) + EPS) * g


def _in_proj_body(x_ref, g_ref, w_ref, o_ref, h_ref):
    @pl.when(pl.program_id(1) == 0)
    def _():
        h_ref[...] = _rms(x_ref[...], g_ref[...]).astype(BF16)

    o_ref[...] = jnp.dot(h_ref[...], w_ref[...], preferred_element_type=F32).astype(o_ref.dtype)


def _in_proj(x, g, w):
    s = x.shape[0]
    return pl.pallas_call(
        _in_proj_body,
        out_shape=jax.ShapeDtypeStruct((s, IN_COLS), BF16),
        grid=(s // IN_TM, IN_COLS // IN_TN),
        in_specs=[
            pl.BlockSpec((IN_TM, D_MODEL), lambda i, j: (i, 0)),
            pl.BlockSpec((1, D_MODEL), lambda i, j: (0, 0)),
            pl.BlockSpec((D_MODEL, IN_TN), lambda i, j: (0, j)),
        ],
        out_specs=pl.BlockSpec((IN_TM, IN_TN), lambda i, j: (i, j)),
        scratch_shapes=[pltpu.VMEM((IN_TM, D_MODEL), BF16)],
        compiler_params=_params("arbitrary", "arbitrary"),
        name="in_proj",
    )(x, g, w)


def _rotate(x, cos, sin_signed, first_half, half):
    w = x.shape[-1]
    partner = jnp.where(first_half, pltpu.roll(x, w - half, 1), pltpu.roll(x, half, 1))
    return x * cos + partner * sin_signed


def _qk_prep_body(pos_ref, invf_ref, dqk_ref, v_ref, rqk_ref, gain_ref, ones_ref,
                  qt_out, k_out, vt_out, rqk_out):
    tm = pos_ref.shape[0]
    cw = MXU_DIM
    ang = pos_ref[...].astype(F32) * invf_ref[...]
    c, s = jnp.cos(ang), jnp.sin(ang)
    lane = lax.broadcasted_iota(jnp.int32, (tm, LANES), 1)
    low = lane < 64
    c_sw, s_sw = pltpu.roll(c, 64, 1), pltpu.roll(s, 64, 1)
    l64 = lane % 64
    ret_first = l64 < RET_QK_DIM // 2
    da_first = l64 < DA_ROT_DIM // 2
    c_r = jnp.where(low, c, c_sw)
    s_r = jnp.where(low, s, s_sw)
    s_r = jnp.where(ret_first, -s_r, s_r)
    c_d = jnp.where(low, c_sw, c)
    s_d = jnp.where(low, s_sw, s)
    s_d = jnp.where(da_first, -s_d, s_d)

    def wide(t):
        return jnp.concatenate([t, t], axis=1)

    c_r, s_r, c_d, s_d = wide(c_r), wide(s_r), wide(c_d), wide(s_d)
    ret_first, da_first = wide(ret_first), wide(da_first)

    ones_bd = ones_ref[...]
    n_q = DA_K_OFF // cw
    for ch in range(dqk_ref.shape[1] // cw):
        sl = slice(ch * cw, (ch + 1) * cw)
        x = dqk_ref[:, sl].astype(F32)
        ss = jnp.dot((x * x).astype(BF16), ones_bd, preferred_element_type=F32)
        xn = x * lax.rsqrt(ss * (1.0 / DA_QK_DIM) + EPS) * gain_ref[:, sl]
        y = _rotate(xn, c_d, s_d, da_first, DA_ROT_DIM // 2)
        if ch < n_q:
            qt_out[sl, :] = (y * (DA_QK_DIM ** -0.5 * LOG2E)).T.astype(BF16)
        else:
            k_out[:, (ch - n_q) * cw:(ch - n_q + 1) * cw] = y.astype(BF16)
    for h in range(DA_HEADS):
        vt_out[h, 0:DA_V_DIM, :] = v_ref[:, h * DA_V_DIM:(h + 1) * DA_V_DIM].astype(F32).T.astype(BF16)
        vt_out[h, DA_V_DIM:DA_VT_ROWS, :] = jnp.ones((DA_VT_ROWS - DA_V_DIM, tm), BF16)
    for ch in range(rqk_ref.shape[1] // cw):
        sl = slice(ch * cw, (ch + 1) * cw)
        x = rqk_ref[:, sl].astype(F32)
        y = _rotate(x, c_r, s_r, ret_first, RET_QK_DIM // 2)
        if ch * cw >= RET_K_OFF - RET_Q_OFF:
            y = y * (RET_QK_DIM ** -0.5)
        rqk_out[:, sl] = y.astype(BF16)


def _qk_prep(proj, positions, da_q_g, da_k_g):
    s = proj.shape[0]
    tm = PREP_TM
    half_r = RET_QK_DIM // 2
    half_d = DA_ROT_DIM // 2
    inv_r = RET_ROT_BASE ** (-jnp.arange(half_r, dtype=F32) * 2.0 / RET_QK_DIM)
    inv_d = ROPE_THETA ** (-jnp.arange(half_d, dtype=F32) * 2.0 / DA_ROT_DIM)
    inv = jnp.concatenate([inv_r, inv_r, inv_d, inv_d, jnp.zeros((64 - DA_ROT_DIM,), F32)]).reshape(1, LANES)
    gain = jnp.concatenate([jnp.tile(da_q_g.astype(F32), 2 * DA_HEADS),
                            jnp.tile(da_k_g.astype(F32), 2 * DA_HEADS)]).reshape(1, 2048)
    blk = jnp.arange(MXU_DIM) // DA_QK_DIM
    ones_bd = (blk[:, None] == blk[None, :]).astype(BF16)
    return pl.pallas_call(
        _qk_prep_body,
        out_shape=(jax.ShapeDtypeStruct((1024, s), BF16),
                   jax.ShapeDtypeStruct((s, 1024), BF16),
                   jax.ShapeDtypeStruct((DA_HEADS, DA_VT_ROWS, s), BF16),
                   jax.ShapeDtypeStruct((s, 1024), BF16)),
        grid=(s // tm,),
        in_specs=[
            pl.BlockSpec((tm, 1), lambda i: (i, 0)),
            pl.BlockSpec((1, LANES), lambda i: (0, 0)),
            pl.BlockSpec((tm, 2048), lambda i: (i, DA_Q_OFF // 2048)),
            pl.BlockSpec((tm, 1024), lambda i: (i, DA_V_OFF // 1024)),
            pl.BlockSpec((tm, 1024), lambda i: (i, RET_Q_OFF // 1024)),
            pl.BlockSpec((1, 2048), lambda i: (0, 0)),
            pl.BlockSpec((MXU_DIM, MXU_DIM), lambda i: (0, 0)),
        ],
        out_specs=(pl.BlockSpec((1024, tm), lambda i: (0, i)),
                   pl.BlockSpec((tm, 1024), lambda i: (i, 0)),
                   pl.BlockSpec((DA_HEADS, DA_VT_ROWS, tm), lambda i: (0, 0, i)),
                   pl.BlockSpec((tm, 1024), lambda i: (i, 0))),
        compiler_params=_params("arbitrary"),
        name="qk_prep",
    )(positions.reshape(s, 1), inv, proj, proj, proj, gain, ones_bd)


def _diff_attn_body(lam_ref, g_ref, qt_ref, k_ref, vt_ref, o_ref, sa_ref, sb_ref, mxa_ref, mxb_ref,
                    m_ref, acc_ref, *, lambda_init):
    tq = qt_ref.shape[1]
    tk = sa_ref.shape[1]
    ratio = tq // tk
    qi = pl.program_id(1)
    last = k_ref.shape[0] // tk - 1
    row = lax.broadcasted_iota(jnp.int32, (LANES, tq), 0)
    qf = qt_ref[...].astype(F32)
    qz = (jnp.where(row < DA_QK_DIM, qf, 0.0).astype(BF16),
          jnp.where(row >= DA_QK_DIM, qf, 0.0).astype(BF16))
    m_ref[...] = jnp.full(m_ref.shape, NEG_BIG, F32)
    acc_ref[...] = jnp.zeros(acc_ref.shape, F32)
    rel = (lax.broadcasted_iota(jnp.int32, (tk, tq), 0) - lax.broadcasted_iota(jnp.int32, (tk, tq), 1))

    def scores(j, s_ref, mx_ref, masked):
        off = pl.multiple_of(jnp.minimum(j, last) * tk, tk)
        ks = k_ref[pl.ds(off, tk), :]
        for mp in range(2):
            sc = jnp.dot(ks, qz[mp], preferred_element_type=F32)
            if masked:
                sc = jnp.where(rel <= qi * tq - j * tk, sc, NEG_BIG)
            s_ref[mp] = sc
            mx_ref[mp] = jnp.max(sc, axis=0, keepdims=True)

    def update(j, s_ref, mx_ref):
        off = pl.multiple_of(jnp.minimum(j, last) * tk, tk)
        vts = vt_ref[:, pl.ds(off, tk)]
        for mp in range(2):
            m_old = m_ref[mp]
            m_new = jnp.maximum(m_old, mx_ref[mp])
            alpha = jnp.exp2(m_old - m_new)
            p = jnp.exp2(s_ref[mp] - m_new).astype(BF16)
            acc_ref[mp] = alpha * acc_ref[mp] + jnp.dot(vts, p, preferred_element_type=F32)
            m_ref[mp] = m_new

    scores(0, sa_ref, mxa_ref, True)
    n_pairs = jnp.maximum(qi * ratio - 1, 0) // 2

    def pair(t, carry):
        scores(2 * t + 1, sb_ref, mxb_ref, False)
        update(2 * t, sa_ref, mxa_ref)
        scores(2 * t + 2, sa_ref, mxa_ref, False)
        update(2 * t + 1, sb_ref, mxb_ref)
        return carry

    lax.fori_loop(0, n_pairs, pair, 0)
    e0 = 2 * n_pairs
    bufs = ((sa_ref, mxa_ref), (sb_ref, mxb_ref))
    for r in range(1, ratio + 1):
        scores(e0 + r, *bufs[r % 2], True)
        update(e0 + r - 1, *bufs[(r - 1) % 2])
    has_extra = (qi + 1) * ratio - 1 - e0 > ratio

    @pl.when(has_extra)
    def _():
        scores(e0 + ratio + 1, *bufs[(ratio + 1) % 2], True)
        update(e0 + ratio, *bufs[ratio % 2])
        update(e0 + ratio + 1, *bufs[(ratio + 1) % 2])

    @pl.when(jnp.logical_not(has_extra))
    def _():
        update(e0 + ratio, *bufs[ratio % 2])

    lq1, lk1 = lam_ref[0:1, :], lam_ref[1:2, :]
    lq2, lk2 = lam_ref[2:3, :], lam_ref[3:4, :]
    lam = (jnp.exp(jnp.sum(lq1 * lk1, axis=-1, keepdims=True))
           - jnp.exp(jnp.sum(lq2 * lk2, axis=-1, keepdims=True)) + lambda_init)
    a1, a2 = acc_ref[0], acc_ref[1]
    dv = DA_V_DIM
    ot = a1[0:dv] / a1[dv:dv + 1] - lam * (a2[0:dv] / a2[dv:dv + 1])
    ot = ot * lax.rsqrt(jnp.mean(ot * ot, axis=0, keepdims=True) + EPS)
    o_ref[...] = (ot.T * g_ref[...] * (1.0 - lambda_init)).astype(o_ref.dtype)


def _diff_attn(qt, k, vt, lam_params, subln_g, lambda_init):
    s = k.shape[0]
    tq = DA_TQ
    body = functools.partial(_diff_attn_body, lambda_init=lambda_init)
    return pl.pallas_call(
        body,
        out_shape=jax.ShapeDtypeStruct((s, DA_HEADS * DA_V_DIM), BF16),
        grid=(DA_HEADS, s // tq),
        in_specs=[
            pl.BlockSpec((4, DA_QK_DIM), lambda h, i: (0, 0)),
            pl.BlockSpec((1, DA_V_DIM), lambda h, i: (0, 0)),
            pl.BlockSpec((LANES, tq), lambda h, i: (h, i)),
            pl.BlockSpec((s, LANES), lambda h, i: (0, h)),
            pl.BlockSpec((None, DA_VT_ROWS, s), lambda h, i: (h, 0, 0)),
        ],
        out_specs=pl.BlockSpec((tq, DA_V_DIM), lambda h, i: (i, h)),
        scratch_shapes=[pltpu.VMEM((2, DA_TK, tq), F32), pltpu.VMEM((2, DA_TK, tq), F32),
                        pltpu.VMEM((2, 1, tq), F32), pltpu.VMEM((2, 1, tq), F32),
                        pltpu.VMEM((2, 1, tq), F32), pltpu.VMEM((2, DA_VT_ROWS, tq), F32)],
        compiler_params=_params("arbitrary", "arbitrary"),
        name="diff_attn",
    )(lam_params, subln_g, qt, k, vt)


def _retention_body(q_ref, k_ref, v_ref, gate_ref, g_ref, o_ref, state_ref, decay_ref, qdec_ref, kdec_ref):
    c = q_ref.shape[0]
    log_gamma = [math.log(1.0 - 2.0 ** (-5.0 - h)) for h in range(RET_HEADS)]

    @pl.when(pl.program_id(0) == 0)
    def _():
        state_ref[...] = jnp.zeros(state_ref.shape, F32)
        row = lax.broadcasted_iota(jnp.int32, (c, c), 0)
        col = lax.broadcasted_iota(jnp.int32, (c, c), 1)
        rel = (row - col).astype(F32)
        idx = lax.broadcasted_iota(jnp.int32, (c, 1), 0).astype(F32)
        for h in range(RET_HEADS):
            decay_ref[h] = jnp.where(rel >= 0, jnp.exp(log_gamma[h] * jnp.maximum(rel, 0.0)), 0.0)
            qdec_ref[h] = jnp.exp(log_gamma[h] * (idx + 1.0))
            kdec_ref[h] = jnp.exp(log_gamma[h] * (c - 1.0 - idx))

    lane = lax.broadcasted_iota(jnp.int32, (c, LANES), 1)
    for h in range(RET_HEADS):
        pair = slice((h // 2) * LANES, (h // 2 + 1) * LANES)
        mine = (lane >= RET_QK_DIM) if h % 2 else (lane < RET_QK_DIM)
        qh = jnp.where(mine, q_ref[:, pair].astype(F32), 0.0)
        kh = jnp.where(mine, k_ref[:, pair].astype(F32), 0.0)
        vh = v_ref[:, h * RET_V_DIM:(h + 1) * RET_V_DIM]
        qb = qh.astype(BF16)
        scores = lax.dot_general(qb, kh.astype(BF16), (((1,), (1,)), ((), ())), preferred_element_type=F32)
        scores = scores * decay_ref[h]
        o_in = jnp.dot(scores.astype(BF16), vh, preferred_element_type=F32)
        state = state_ref[h]
        o_cross = jnp.dot(qb, state.astype(BF16), preferred_element_type=F32) * qdec_ref[h]
        kd = (kh * kdec_ref[h]).astype(BF16)
        state_ref[h] = state * math.exp(log_gamma[h] * c) + lax.dot_general(
            kd, vh, (((0,), (0,)), ((), ())), preferred_element_type=F32)
        o = _rms(o_in + o_cross, g_ref[...])
        gate = gate_ref[:, h * RET_V_DIM:(h + 1) * RET_V_DIM].astype(F32)
        o_ref[:, h * RET_V_DIM:(h + 1) * RET_V_DIM] = (gate * jax.nn.sigmoid(gate) * o).astype(o_ref.dtype)


def _retention(rqk, proj, norm_g):
    s = rqk.shape[0]
    c = RET_CHUNK
    w = RET_HEADS * RET_V_DIM
    return pl.pallas_call(
        _retention_body,
        out_shape=jax.ShapeDtypeStruct((s, w), BF16),
        grid=(s // c,),
        in_specs=[
            pl.BlockSpec((c, 512), lambda i: (i, 0)),
            pl.BlockSpec((c, 512), lambda i: (i, 1)),
            pl.BlockSpec((c, w), lambda i: (i, RET_V_OFF // w)),
            pl.BlockSpec((c, w), lambda i: (i, RET_G_OFF // w)),
            pl.BlockSpec((1, RET_V_DIM), lambda i: (0, 0)),
        ],
        out_specs=pl.BlockSpec((c, w), lambda i: (i, 0)),
        scratch_shapes=[pltpu.VMEM((RET_HEADS, LANES, RET_V_DIM), F32),
                        pltpu.VMEM((RET_HEADS, c, c), F32),
                        pltpu.VMEM((RET_HEADS, c, 1), F32),
                        pltpu.VMEM((RET_HEADS, c, 1), F32)],
        compiler_params=_params("arbitrary"),
        name="retention",
    )(rqk, rqk, proj, proj, norm_g)


def _mem_kv_body(mem_ref, g_ref, w_ref, kg_ref, o_ref):
    m = _rms(mem_ref[...], g_ref[...]).astype(BF16)
    kv = jnp.dot(m, w_ref[...], preferred_element_type=F32)
    kn = _rms(kv, kg_ref[...]) * (MEM_HEAD_DIM ** -0.5)
    o_ref[...] = jnp.where(pl.program_id(0) < MEM_HEADS, kn, kv).astype(o_ref.dtype)


def _mem_kv(mem, mem_norm_g, w_mem_kv, k_g):
    n = mem.shape[0]
    hd = MEM_HEAD_DIM
    return pl.pallas_call(
        _mem_kv_body,
        out_shape=jax.ShapeDtypeStruct((n, 2 * MEM_HEADS * hd), BF16),
        grid=(2 * MEM_HEADS,),
        in_specs=[
            pl.BlockSpec((n, D_MODEL), lambda j: (0, 0)),
            pl.BlockSpec((1, D_MODEL), lambda j: (0, 0)),
            pl.BlockSpec((D_MODEL, hd), lambda j: (0, j)),
            pl.BlockSpec((1, hd), lambda j: (0, 0)),
        ],
        out_specs=pl.BlockSpec((n, hd), lambda j: (0, j)),
        compiler_params=_params("arbitrary"),
        name="mem_kv",
    )(mem, mem_norm_g, w_mem_kv, k_g)


def _merge_body(x_ref, oda_ref, oret_ref, mq_ref, gda_ref, gret_ref, gmem_ref, kv_ref, qg_ref,
                wda_ref, wret_ref, wmem_ref, wout_ref, o_ref, omem_ref):
    hd = MEM_HEAD_DIM

    @pl.when(pl.program_id(1) == 0)
    def _():
        o_ref[...] = x_ref[...]
        for h in range(MEM_HEADS):
            q = _rms(mq_ref[:, h * hd:(h + 1) * hd].astype(F32), qg_ref[...]).astype(BF16)
            k = kv_ref[:, h * hd:(h + 1) * hd]
            v = kv_ref[:, (MEM_HEADS + h) * hd:(MEM_HEADS + h + 1) * hd]
            sc = lax.dot_general(q, k, (((1,), (1,)), ((), ())), preferred_element_type=F32)
            p = jnp.exp(sc - jnp.max(sc, axis=-1, keepdims=True))
            o = jnp.dot(p.astype(BF16), v, preferred_element_type=F32) / jnp.sum(p, axis=-1, keepdims=True)
            omem_ref[:, h * hd:(h + 1) * hd] = o.astype(BF16)

    def branch(gate_ref, act, w_ref):
        return jax.nn.sigmoid(gate_ref[...].astype(F32)) * jnp.dot(act, w_ref[...], preferred_element_type=F32)

    merged = (branch(gda_ref, oda_ref[...], wda_ref) + branch(gret_ref, oret_ref[...], wret_ref)
              + branch(gmem_ref, omem_ref[...], wmem_ref))
    o_ref[...] += jnp.dot(merged.astype(BF16), wout_ref[...], preferred_element_type=F32)


def _merge(x, o_da, o_ret, proj, kv_mem, mem_q_g, w_o_da, w_o_ret, w_o_mem, w_out):
    s = x.shape[0]
    tm, tn = MERGE_TM, MERGE_TN
    n_mem = kv_mem.shape[0]
    bw = 1024
    gate_blk = GATE_OFF // tn
    per_gate = D_MODEL // tn
    row = lambda i, j: (i, 0)
    return pl.pallas_call(
        _merge_body,
        out_shape=jax.ShapeDtypeStruct((s, D_MODEL), F32),
        grid=(s // tm, D_MODEL // tn),
        in_specs=[
            pl.BlockSpec((tm, D_MODEL), row),
            pl.BlockSpec((tm, bw), row),
            pl.BlockSpec((tm, bw), row),
            pl.BlockSpec((tm, bw), lambda i, j: (i, MEM_Q_OFF // bw)),
            pl.BlockSpec((tm, tn), lambda i, j: (i, gate_blk + j)),
            pl.BlockSpec((tm, tn), lambda i, j: (i, gate_blk + per_gate + j)),
            pl.BlockSpec((tm, tn), lambda i, j: (i, gate_blk + 2 * per_gate + j)),
            pl.BlockSpec((n_mem, 2 * bw), lambda i, j: (0, 0)),
            pl.BlockSpec((1, MEM_HEAD_DIM), lambda i, j: (0, 0)),
            pl.BlockSpec((bw, tn), lambda i, j: (0, j)),
            pl.BlockSpec((bw, tn), lambda i, j: (0, j)),
            pl.BlockSpec((bw, tn), lambda i, j: (0, j)),
            pl.BlockSpec((tn, D_MODEL), lambda i, j: (j, 0)),
        ],
        out_specs=pl.BlockSpec((tm, D_MODEL), row),
        scratch_shapes=[pltpu.VMEM((tm, bw), BF16)],
        compiler_params=_params("arbitrary", "arbitrary"),
        name="merge",
    )(x, o_da, o_ret, proj, proj, proj, proj, kv_mem, mem_q_g, w_o_da, w_o_ret, w_o_mem, w_out)


def _ffn_body(x_ref, g_ref, wg_ref, wu_ref, wd_ref, o_ref, h_ref):
    @pl.when(pl.program_id(1) == 0)
    def _():
        x = x_ref[...]
        o_ref[...] = x
        h_ref[...] = _rms(x, g_ref[...]).astype(BF16)

    h = h_ref[...]
    gate = jnp.dot(h, wg_ref[...], preferred_element_type=F32)
    up = jnp.dot(h, wu_ref[...], preferred_element_type=F32)
    act = (gate * jax.nn.sigmoid(gate) * up).astype(BF16)
    o_ref[...] += jnp.dot(act, wd_ref[...], preferred_element_type=F32)


def _ffn(x, g, w_gate, w_up, w_down):
    s = x.shape[0]
    tm, tf = FFN_TM, FFN_TF
    return pl.pallas_call(
        _ffn_body,
        out_shape=jax.ShapeDtypeStruct((s, D_MODEL), F32),
        grid=(s // tm, D_FF // tf),
        in_specs=[
            pl.BlockSpec((tm, D_MODEL), lambda i, j: (i, 0)),
            pl.BlockSpec((1, D_MODEL), lambda i, j: (0, 0)),
            pl.BlockSpec((D_MODEL, tf), lambda i, j: (0, j)),
            pl.BlockSpec((D_MODEL, tf), lambda i, j: (0, j)),
            pl.BlockSpec((tf, D_MODEL), lambda i, j: (j, 0)),
        ],
        out_specs=pl.BlockSpec((tm, D_MODEL), lambda i, j: (i, 0)),
        scratch_shapes=[pltpu.VMEM((tm, D_MODEL), BF16)],
        compiler_params=_params("arbitrary", "arbitrary"),
        name="ffn",
    )(x, g, w_gate, w_up, w_down)


def _layer(x, mem, positions, l, attn_norm_g, w_in, da_q_norm_g, da_k_norm_g, da_lambda_q1, da_lambda_k1,
           da_lambda_q2, da_lambda_k2, da_subln_g, ret_norm_g, mem_norm_g, w_mem_kv, mem_q_norm_g,
           mem_k_norm_g, w_o_da, w_o_ret, w_o_mem, w_out, ffn_norm_g, w_ffn_gate, w_ffn_up, w_ffn_down):
    lambda_init = 0.8 - 0.6 * math.exp(-0.3 * l)
    row = lambda a: a.astype(F32).reshape(1, -1)
    proj = _in_proj(x, row(attn_norm_g), w_in.astype(BF16))
    da_qt, da_k, da_vt, rqk = _qk_prep(proj, positions, da_q_norm_g, da_k_norm_g)
    lam_params = jnp.stack([da_lambda_q1, da_lambda_k1, da_lambda_q2, da_lambda_k2]).astype(F32)
    o_da = _diff_attn(da_qt, da_k, da_vt, lam_params, row(da_subln_g), lambda_init)
    o_ret = _retention(rqk, proj, row(ret_norm_g))
    kv_mem = _mem_kv(mem, row(mem_norm_g), w_mem_kv.astype(BF16), row(mem_k_norm_g))
    x1 = _merge(x, o_da, o_ret, proj, kv_mem, row(mem_q_norm_g), w_o_da.astype(BF16),
                w_o_ret.astype(BF16), w_o_mem.astype(BF16), w_out.astype(BF16))
    return _ffn(x1, row(ffn_norm_g), w_ffn_gate.astype(BF16), w_ffn_up.astype(BF16),
                w_ffn_down.astype(BF16))


def kernel(x, mem, positions, attn_norm_g, w_in, da_q_norm_g, da_k_norm_g, da_lambda_q1, da_lambda_k1,
           da_lambda_q2, da_lambda_k2, da_subln_g, ret_norm_g, mem_norm_g, w_mem_kv, mem_q_norm_g,
           mem_k_norm_g, w_o_da, w_o_ret, w_o_mem, w_out, ffn_norm_g, w_ffn_gate, w_ffn_up, w_ffn_down):
    batch, depth = x.shape[0], w_in.shape[0]
    outs = []
    for b in range(batch):
        xb = x[b]
        for l in range(depth):
            xb = _layer(xb, mem[b], positions[b], l, attn_norm_g[l], w_in[l], da_q_norm_g[l], da_k_norm_g[l],
                        da_lambda_q1[l], da_lambda_k1[l], da_lambda_q2[l], da_lambda_k2[l], da_subln_g[l],
                        ret_norm_g[l], mem_norm_g[l], w_mem_kv[l], mem_q_norm_g[l], mem_k_norm_g[l],
                        w_o_da[l], w_o_ret[l], w_o_mem[l], w_out[l], ffn_norm_g[l], w_ffn_gate[l],
                        w_ffn_up[l], w_ffn_down[l])
        outs.append(xb)
    return jnp.stack(outs)
```

```python
import functools
import math

import jax
import jax.numpy as jnp
from jax import lax
from jax.experimental import pallas as pl
from jax.experimental.pallas import tpu as pltpu

F32 = jnp.float32
BF16 = jnp.bfloat16

D_MODEL = 2048
EPS = 1e-6
LOG2E = math.log2(math.e)

DA_HEADS = 8
DA_QK_DIM = 64
DA_V_DIM = 128
DA_ROT_DIM = 16
ROPE_THETA = 500000.0

RET_HEADS = 8
RET_QK_DIM = 64
RET_V_DIM = 128
RET_ROT_BASE = 10000.0

MEM_HEADS = 4
MEM_HEAD_DIM = 256

D_FF = 5632

W_IN_GATE_OFF = 7168
GATE_OFF = 0
DA_Q_OFF = 6144
DA_K_OFF = 7168
DA_V_OFF = 8192
RET_Q_OFF = 9216
RET_K_OFF = 9728
RET_V_OFF = 10240
RET_G_OFF = 11264
MEM_Q_OFF = 12288
IN_COLS = 13312

LANES = 128
BF16_SUBLANES = 16
MXU_DIM = 256
VMEM_LIMIT = 60 * 1024 * 1024
NEG_BIG = -1e30
DA_VT_ROWS = DA_V_DIM + BF16_SUBLANES

IN_TM, IN_TN = 1024, 1024
PREP_TM = 512
DA_TQ, DA_TK = 512, 512
RET_CHUNK = 256
MERGE_TM = 256
FFN_TM, FFN_TF = 1024, 512


def _params(*sem):
    return pltpu.CompilerParams(dimension_semantics=sem, vmem_limit_bytes=VMEM_LIMIT)


def _rms(x, g):
    return x * lax.rsqrt(jnp.mean(x * x, axis=-1, keepdims=True) + EPS) * g


def _in_proj_body(x_ref, g_ref, w_ref, o_ref, h_ref):
    @pl.when(pl.program_id(1) == 0)
    def _():
        h_ref[...] = _rms(x_ref[...], g_ref[...]).astype(BF16)

    o_ref[...] = jnp.dot(h_ref[...], w_ref[...], preferred_element_type=F32).astype(o_ref.dtype)


def _in_proj(x, g, w):
    s = x.shape[0]
    return pl.pallas_call(
        _in_proj_body,
        out_shape=jax.ShapeDtypeStruct((s, IN_COLS), BF16),
        grid=(s // IN_TM, IN_COLS // IN_TN),
        in_specs=[
            pl.BlockSpec((IN_TM, D_MODEL), lambda i, j: (i, 0)),
            pl.BlockSpec((1, D_MODEL), lambda i, j: (0, 0)),
            pl.BlockSpec((D_MODEL, IN_TN), lambda i, j: (0, j)),
        ],
        out_specs=pl.BlockSpec((IN_TM, IN_TN), lambda i, j: (i, j)),
        scratch_shapes=[pltpu.VMEM((IN_TM, D_MODEL), BF16)],
        compiler_params=_params("arbitrary", "arbitrary"),
        name="in_proj",
    )(x, g, w)


def _rotate(x, cos, sin_signed, first_half, half):
    w = x.shape[-1]
    partner = jnp.where(first_half, pltpu.roll(x, w - half, 1), pltpu.roll(x, half, 1))
    return x * cos + partner * sin_signed


def _qk_prep_body(pos_ref, invf_ref, dqk_ref, v_ref, rqk_ref, gain_ref, ones_ref,
                  qt_out, k_out, vt_out, rq_out, rkt_out):
    tm = pos_ref.shape[0]
    cw = MXU_DIM
    ang = pos_ref[...].astype(F32) * invf_ref[...]
    c, s = jnp.cos(ang), jnp.sin(ang)
    lane = lax.broadcasted_iota(jnp.int32, (tm, LANES), 1)
    low = lane < 64
    c_sw, s_sw = pltpu.roll(c, 64, 1), pltpu.roll(s, 64, 1)
    l64 = lane % 64
    ret_first = l64 < RET_QK_DIM // 2
    da_first = l64 < DA_ROT_DIM // 2
    c_r = jnp.where(low, c, c_sw)
    s_r = jnp.where(low, s, s_sw)
    s_r = jnp.where(ret_first, -s_r, s_r)
    c_d = jnp.where(low, c_sw, c)
    s_d = jnp.where(low, s_sw, s)
    s_d = jnp.where(da_first, -s_d, s_d)

    def wide(t):
        return jnp.concatenate([t, t], axis=1)

    c_r, s_r, c_d, s_d = wide(c_r), wide(s_r), wide(c_d), wide(s_d)
    ret_first, da_first = wide(ret_first), wide(da_first)

    ones_bd = ones_ref[...]
    n_q = (DA_K_OFF - DA_Q_OFF) // cw
    for ch in range(dqk_ref.shape[1] // cw):
        sl = slice(ch * cw, (ch + 1) * cw)
        x = dqk_ref[:, sl].astype(F32)
        ss = jnp.dot((x * x).astype(BF16), ones_bd, preferred_element_type=F32)
        xn = x * lax.rsqrt(ss * (1.0 / DA_QK_DIM) + EPS) * gain_ref[:, sl]
        y = _rotate(xn, c_d, s_d, da_first, DA_ROT_DIM // 2)
        if ch < n_q:
            qt_out[sl, :] = (y * (DA_QK_DIM ** -0.5 * LOG2E)).T.astype(BF16)
        else:
            k_out[:, (ch - n_q) * cw:(ch - n_q + 1) * cw] = y.astype(BF16)
    for h in range(DA_HEADS):
        vt_out[h, 0:DA_V_DIM, :] = v_ref[:, h * DA_V_DIM:(h + 1) * DA_V_DIM].astype(F32).T.astype(BF16)
        vt_out[h, DA_V_DIM:DA_VT_ROWS, :] = jnp.ones((DA_VT_ROWS - DA_V_DIM, tm), BF16)
    for ch in range(rqk_ref.shape[1] // cw):
        sl = slice(ch * cw, (ch + 1) * cw)
        x = rqk_ref[:, sl].astype(F32)
        y = _rotate(x, c_r, s_r, ret_first, RET_QK_DIM // 2)
        if ch * cw >= RET_K_OFF - RET_Q_OFF:
            y = y * (RET_QK_DIM ** -0.5)
        is_k = ch * cw >= RET_K_OFF - RET_Q_OFF
        for half in range(cw // LANES):
            yp = y[:, half * LANES:(half + 1) * LANES]
            slots = (jnp.where(low, yp, 0.0), jnp.where(low, pltpu.roll(yp, 64, 1), 0.0))
            for sub, piece in enumerate(slots):
                base = ((ch * cw // LANES + half) * 2 + sub) * LANES
                if is_k:
                    base -= RET_HEADS * LANES
                    rkt_out[base:base + LANES, :] = piece.T.astype(BF16)
                else:
                    rq_out[:, base:base + LANES] = piece.astype(BF16)


def _qk_prep(proj, positions, da_q_g, da_k_g):
    s = proj.shape[0]
    tm = PREP_TM
    half_r = RET_QK_DIM // 2
    half_d = DA_ROT_DIM // 2
    inv_r = RET_ROT_BASE ** (-jnp.arange(half_r, dtype=F32) * 2.0 / RET_QK_DIM)
    inv_d = ROPE_THETA ** (-jnp.arange(half_d, dtype=F32) * 2.0 / DA_ROT_DIM)
    inv = jnp.concatenate([inv_r, inv_r, inv_d, inv_d, jnp.zeros((64 - DA_ROT_DIM,), F32)]).reshape(1, LANES)
    gain = jnp.concatenate([jnp.tile(da_q_g.astype(F32), 2 * DA_HEADS),
                            jnp.tile(da_k_g.astype(F32), 2 * DA_HEADS)]).reshape(1, 2048)
    blk = jnp.arange(MXU_DIM) // DA_QK_DIM
    ones_bd = (blk[:, None] == blk[None, :]).astype(BF16)
    return pl.pallas_call(
        _qk_prep_body,
        out_shape=(jax.ShapeDtypeStruct((1024, s), BF16),
                   jax.ShapeDtypeStruct((s, 1024), BF16),
                   jax.ShapeDtypeStruct((DA_HEADS, DA_VT_ROWS, s), BF16),
                   jax.ShapeDtypeStruct((s, RET_HEADS * LANES), BF16),
                   jax.ShapeDtypeStruct((RET_HEADS * LANES, s), BF16)),
        grid=(s // tm,),
        in_specs=[
            pl.BlockSpec((tm, 1), lambda i: (i, 0)),
            pl.BlockSpec((1, LANES), lambda i: (0, 0)),
            pl.BlockSpec((tm, 2048), lambda i: (i, DA_Q_OFF // 2048)),
            pl.BlockSpec((tm, 1024), lambda i: (i, DA_V_OFF // 1024)),
            pl.BlockSpec((tm, 1024), lambda i: (i, RET_Q_OFF // 1024)),
            pl.BlockSpec((1, 2048), lambda i: (0, 0)),
            pl.BlockSpec((MXU_DIM, MXU_DIM), lambda i: (0, 0)),
        ],
        out_specs=(pl.BlockSpec((1024, tm), lambda i: (0, i)),
                   pl.BlockSpec((tm, 1024), lambda i: (i, 0)),
                   pl.BlockSpec((DA_HEADS, DA_VT_ROWS, tm), lambda i: (0, 0, i)),
                   pl.BlockSpec((tm, RET_HEADS * LANES), lambda i: (i, 0)),
                   pl.BlockSpec((RET_HEADS * LANES, tm), lambda i: (0, i))),
        compiler_params=_params("arbitrary"),
        name="qk_prep",
    )(positions.reshape(s, 1), inv, proj, proj, proj, gain, ones_bd)


def _diff_attn_body(lam_ref, g_ref, qt_ref, k_ref, vt_ref, o_ref, sa_ref, sb_ref, mxa_ref, mxb_ref,
                    m_ref, acc_ref, *, lambda_init):
    tq = qt_ref.shape[1]
    tk = sa_ref.shape[1]
    ratio = tq // tk
    qi = pl.program_id(1)
    last = k_ref.shape[0] // tk - 1
    row = lax.broadcasted_iota(jnp.int32, (LANES, tq), 0)
    qf = qt_ref[...].astype(F32)
    qz = (jnp.where(row < DA_QK_DIM, qf, 0.0).astype(BF16),
          jnp.where(row >= DA_QK_DIM, qf, 0.0).astype(BF16))
    m_ref[...] = jnp.full(m_ref.shape, NEG_BIG, F32)
    acc_ref[...] = jnp.zeros(acc_ref.shape, F32)
    rel = (lax.broadcasted_iota(jnp.int32, (tk, tq), 0) - lax.broadcasted_iota(jnp.int32, (tk, tq), 1))

    def scores(j, s_ref, mx_ref, masked):
        off = pl.multiple_of(jnp.minimum(j, last) * tk, tk)
        ks = k_ref[pl.ds(off, tk), :]
        for mp in range(2):
            sc = jnp.dot(ks, qz[mp], preferred_element_type=F32)
            if masked:
                sc = jnp.where(rel <= qi * tq - j * tk, sc, NEG_BIG)
            s_ref[mp] = sc
            mx_ref[mp] = jnp.max(sc, axis=0, keepdims=True)

    def update(j, s_ref, mx_ref):
        off = pl.multiple_of(jnp.minimum(j, last) * tk, tk)
        vts = vt_ref[:, pl.ds(off, tk)]
        for mp in range(2):
            m_old = m_ref[mp]
            m_new = jnp.maximum(m_old, mx_ref[mp])
            alpha = jnp.exp2(m_old - m_new)
            p = jnp.exp2(s_ref[mp] - m_new).astype(BF16)
            acc_ref[mp] = alpha * acc_ref[mp] + jnp.dot(vts, p, preferred_element_type=F32)
            m_ref[mp] = m_new

    scores(0, sa_ref, mxa_ref, True)
    n_pairs = jnp.maximum(qi * ratio - 1, 0) // 2

    def pair(t, carry):
        scores(2 * t + 1, sb_ref, mxb_ref, False)
        update(2 * t, sa_ref, mxa_ref)
        scores(2 * t + 2, sa_ref, mxa_ref, False)
        update(2 * t + 1, sb_ref, mxb_ref)
        return carry

    lax.fori_loop(0, n_pairs, pair, 0)
    e0 = 2 * n_pairs
    bufs = ((sa_ref, mxa_ref), (sb_ref, mxb_ref))
    for r in range(1, ratio + 1):
        scores(e0 + r, *bufs[r % 2], True)
        update(e0 + r - 1, *bufs[(r - 1) % 2])
    has_extra = (qi + 1) * ratio - 1 - e0 > ratio

    @pl.when(has_extra)
    def _():
        scores(e0 + ratio + 1, *bufs[(ratio + 1) % 2], True)
        update(e0 + ratio, *bufs[ratio % 2])
        update(e0 + ratio + 1, *bufs[(ratio + 1) % 2])

    @pl.when(jnp.logical_not(has_extra))
    def _():
        update(e0 + ratio, *bufs[ratio % 2])

    lq1, lk1 = lam_ref[0:1, :], lam_ref[1:2, :]
    lq2, lk2 = lam_ref[2:3, :], lam_ref[3:4, :]
    lam = (jnp.exp(jnp.sum(lq1 * lk1, axis=-1, keepdims=True))
           - jnp.exp(jnp.sum(lq2 * lk2, axis=-1, keepdims=True)) + lambda_init)
    a1, a2 = acc_ref[0], acc_ref[1]
    dv = DA_V_DIM
    ot = a1[0:dv] / a1[dv:dv + 1] - lam * (a2[0:dv] / a2[dv:dv + 1])
    ot = ot * lax.rsqrt(jnp.mean(ot * ot, axis=0, keepdims=True) + EPS)
    o_ref[...] = (ot.T * g_ref[...] * (1.0 - lambda_init)).astype(o_ref.dtype)


def _diff_attn(qt, k, vt, lam_params, subln_g, lambda_init):
    s = k.shape[0]
    tq = DA_TQ
    body = functools.partial(_diff_attn_body, lambda_init=lambda_init)
    return pl.pallas_call(
        body,
        out_shape=jax.ShapeDtypeStruct((s, DA_HEADS * DA_V_DIM), BF16),
        grid=(DA_HEADS, s // tq),
        in_specs=[
            pl.BlockSpec((4, DA_QK_DIM), lambda h, i: (0, 0)),
            pl.BlockSpec((1, DA_V_DIM), lambda h, i: (0, 0)),
            pl.BlockSpec((LANES, tq), lambda h, i: (h, i)),
            pl.BlockSpec((s, LANES), lambda h, i: (0, h)),
            pl.BlockSpec((None, DA_VT_ROWS, s), lambda h, i: (h, 0, 0)),
        ],
        out_specs=pl.BlockSpec((tq, DA_V_DIM), lambda h, i: (i, h)),
        scratch_shapes=[pltpu.VMEM((2, DA_TK, tq), F32), pltpu.VMEM((2, DA_TK, tq), F32),
                        pltpu.VMEM((2, 1, tq), F32), pltpu.VMEM((2, 1, tq), F32),
                        pltpu.VMEM((2, 1, tq), F32), pltpu.VMEM((2, DA_VT_ROWS, tq), F32)],
        compiler_params=_params("arbitrary", "arbitrary"),
        name="diff_attn",
    )(lam_params, subln_g, qt, k, vt)


def _retention_body(q_ref, k_ref, v_ref, gate_ref, g_ref, o_ref, state_ref, decay_ref, qdec_ref, kdec_ref):
    c = q_ref.shape[0]
    log_gamma = [math.log(1.0 - 2.0 ** (-5.0 - h)) for h in range(RET_HEADS)]

    @pl.when(pl.program_id(0) == 0)
    def _():
        state_ref[...] = jnp.zeros(state_ref.shape, F32)
        row = lax.broadcasted_iota(jnp.int32, (c, c), 0)
        col = lax.broadcasted_iota(jnp.int32, (c, c), 1)
        rel = (row - col).astype(F32)
        idx = lax.broadcasted_iota(jnp.int32, (c, 1), 0).astype(F32)
        idx_row = lax.broadcasted_iota(jnp.int32, (1, c), 1).astype(F32)
        for h in range(RET_HEADS):
            decay_ref[h] = jnp.where(rel >= 0, jnp.exp(log_gamma[h] * jnp.maximum(rel, 0.0)), 0.0)
            qdec_ref[h] = jnp.exp(log_gamma[h] * (idx + 1.0))
            kdec_ref[h] = jnp.exp(log_gamma[h] * (c - 1.0 - idx_row))

    for h in range(RET_HEADS):
        qb = q_ref[:, h * LANES:(h + 1) * LANES]
        kt = k_ref[h * LANES:(h + 1) * LANES, :]
        vh = v_ref[:, h * RET_V_DIM:(h + 1) * RET_V_DIM]
        scores = jnp.dot(qb, kt, preferred_element_type=F32) * decay_ref[h]
        o_in = jnp.dot(scores.astype(BF16), vh, preferred_element_type=F32)
        state = state_ref[h]
        o_cross = jnp.dot(qb, state.astype(BF16), preferred_element_type=F32) * qdec_ref[h]
        kd = (kt.astype(F32) * kdec_ref[h]).astype(BF16)
        state_ref[h] = state * math.exp(log_gamma[h] * c) + jnp.dot(kd, vh, preferred_element_type=F32)
        o = _rms(o_in + o_cross, g_ref[...])
        gate = gate_ref[:, h * RET_V_DIM:(h + 1) * RET_V_DIM].astype(F32)
        o_ref[:, h * RET_V_DIM:(h + 1) * RET_V_DIM] = (gate * jax.nn.sigmoid(gate) * o).astype(o_ref.dtype)


def _retention(rq, rkt, proj, norm_g):
    s = rq.shape[0]
    c = RET_CHUNK
    w = RET_HEADS * RET_V_DIM
    return pl.pallas_call(
        _retention_body,
        out_shape=jax.ShapeDtypeStruct((s, w), BF16),
        grid=(s // c,),
        in_specs=[
            pl.BlockSpec((c, RET_HEADS * LANES), lambda i: (i, 0)),
            pl.BlockSpec((RET_HEADS * LANES, c), lambda i: (0, i)),
            pl.BlockSpec((c, w), lambda i: (i, RET_V_OFF // w)),
            pl.BlockSpec((c, w), lambda i: (i, RET_G_OFF // w)),
            pl.BlockSpec((1, RET_V_DIM), lambda i: (0, 0)),
        ],
        out_specs=pl.BlockSpec((c, w), lambda i: (i, 0)),
        scratch_shapes=[pltpu.VMEM((RET_HEADS, LANES, RET_V_DIM), F32),
                        pltpu.VMEM((RET_HEADS, c, c), F32),
                        pltpu.VMEM((RET_HEADS, c, 1), F32),
                        pltpu.VMEM((RET_HEADS, 1, c), F32)],
        compiler_params=_params("arbitrary"),
        name="retention",
    )(rq, rkt, proj, proj, norm_g)


def _mem_kv_body(mem_ref, g_ref, w_ref, kg_ref, o_ref):
    m = _rms(mem_ref[...], g_ref[...]).astype(BF16)
    kv = jnp.dot(m, w_ref[...], preferred_element_type=F32)
    kn = _rms(kv, kg_ref[...]) * (MEM_HEAD_DIM ** -0.5)
    o_ref[...] = jnp.where(pl.program_id(0) < MEM_HEADS, kn, kv).astype(o_ref.dtype)


def _mem_kv(mem, mem_norm_g, w_mem_kv, k_g):
    n = mem.shape[0]
    hd = MEM_HEAD_DIM
    return pl.pallas_call(
        _mem_kv_body,
        out_shape=jax.ShapeDtypeStruct((n, 2 * MEM_HEADS * hd), BF16),
        grid=(2 * MEM_HEADS,),
        in_specs=[
            pl.BlockSpec((n, D_MODEL), lambda j: (0, 0)),
            pl.BlockSpec((1, D_MODEL), lambda j: (0, 0)),
            pl.BlockSpec((D_MODEL, hd), lambda j: (0, j)),
            pl.BlockSpec((1, hd), lambda j: (0, 0)),
        ],
        out_specs=pl.BlockSpec((n, hd), lambda j: (0, j)),
        compiler_params=_params("arbitrary"),
        name="mem_kv",
    )(mem, mem_norm_g, w_mem_kv, k_g)


def _merge_body(x_ref, oda_ref, oret_ref, mq_ref, gates_ref, kv_ref, qg_ref,
                wda_ref, wret_ref, wmem_ref, wout_ref, o_ref, omem_ref):
    hd = MEM_HEAD_DIM
    for h in range(MEM_HEADS):
        q = _rms(mq_ref[:, h * hd:(h + 1) * hd].astype(F32), qg_ref[...]).astype(BF16)
        k = kv_ref[:, h * hd:(h + 1) * hd]
        v = kv_ref[:, (MEM_HEADS + h) * hd:(MEM_HEADS + h + 1) * hd]
        sc = lax.dot_general(q, k, (((1,), (1,)), ((), ())), preferred_element_type=F32)
        p = jnp.exp(sc - jnp.max(sc, axis=-1, keepdims=True))
        o = jnp.dot(p.astype(BF16), v, preferred_element_type=F32) / jnp.sum(p, axis=-1, keepdims=True)
        omem_ref[:, h * hd:(h + 1) * hd] = o.astype(BF16)

    def branch(b, act, w_ref):
        gate = gates_ref[:, b * D_MODEL:(b + 1) * D_MODEL].astype(F32)
        return jax.nn.sigmoid(gate) * jnp.dot(act, w_ref[...], preferred_element_type=F32)

    merged = (branch(0, oda_ref[...], wda_ref) + branch(1, oret_ref[...], wret_ref)
              + branch(2, omem_ref[...], wmem_ref))
    o_ref[...] = x_ref[...] + jnp.dot(merged.astype(BF16), wout_ref[...], preferred_element_type=F32)


def _merge(x, o_da, o_ret, proj, kv_mem, mem_q_g, w_o_da, w_o_ret, w_o_mem, w_out):
    s = x.shape[0]
    tm = MERGE_TM
    n_mem = kv_mem.shape[0]
    bw = 1024
    gw = 3 * D_MODEL
    row = lambda i: (i, 0)
    resident = lambda shape: pl.BlockSpec(shape, lambda i: (0, 0), pipeline_mode=pl.Buffered(1))
    return pl.pallas_call(
        _merge_body,
        out_shape=jax.ShapeDtypeStruct((s, D_MODEL), F32),
        grid=(s // tm,),
        in_specs=[
            pl.BlockSpec((tm, D_MODEL), row),
            pl.BlockSpec((tm, bw), row),
            pl.BlockSpec((tm, bw), row),
            pl.BlockSpec((tm, bw), lambda i: (i, MEM_Q_OFF // bw)),
            pl.BlockSpec((tm, gw), lambda i: (i, GATE_OFF // gw)),
            resident((n_mem, 2 * bw)),
            resident((1, MEM_HEAD_DIM)),
            resident((bw, D_MODEL)),
            resident((bw, D_MODEL)),
            resident((bw, D_MODEL)),
            resident((D_MODEL, D_MODEL)),
        ],
        out_specs=pl.BlockSpec((tm, D_MODEL), row),
        scratch_shapes=[pltpu.VMEM((tm, bw), BF16)],
        compiler_params=_params("arbitrary"),
        name="merge",
    )(x, o_da, o_ret, proj, proj, kv_mem, mem_q_g, w_o_da, w_o_ret, w_o_mem, w_out)


def _ffn_body(x_ref, g_ref, wg_ref, wu_ref, wd_ref, o_ref, h_ref):
    @pl.when(pl.program_id(1) == 0)
    def _():
        x = x_ref[...]
        o_ref[...] = x
        h_ref[...] = _rms(x, g_ref[...]).astype(BF16)

    h = h_ref[...]
    gate = jnp.dot(h, wg_ref[...], preferred_element_type=F32)
    up = jnp.dot(h, wu_ref[...], preferred_element_type=F32)
    act = (gate * jax.nn.sigmoid(gate) * up).astype(BF16)
    o_ref[...] += jnp.dot(act, wd_ref[...], preferred_element_type=F32)


def _ffn(x, g, w_gate, w_up, w_down):
    s = x.shape[0]
    tm, tf = FFN_TM, FFN_TF
    return pl.pallas_call(
        _ffn_body,
        out_shape=jax.ShapeDtypeStruct((s, D_MODEL), F32),
        grid=(s // tm, D_FF // tf),
        in_specs=[
            pl.BlockSpec((tm, D_MODEL), lambda i, j: (i, 0)),
            pl.BlockSpec((1, D_MODEL), lambda i, j: (0, 0)),
            pl.BlockSpec((D_MODEL, tf), lambda i, j: (0, j)),
            pl.BlockSpec((D_MODEL, tf), lambda i, j: (0, j)),
            pl.BlockSpec((tf, D_MODEL), lambda i, j: (j, 0)),
        ],
        out_specs=pl.BlockSpec((tm, D_MODEL), lambda i, j: (i, 0)),
        scratch_shapes=[pltpu.VMEM((tm, D_MODEL), BF16)],
        compiler_params=_params("arbitrary", "arbitrary"),
        name="ffn",
    )(x, g, w_gate, w_up, w_down)


def _layer(x, mem, positions, l, attn_norm_g, w_in, da_q_norm_g, da_k_norm_g, da_lambda_q1, da_lambda_k1,
           da_lambda_q2, da_lambda_k2, da_subln_g, ret_norm_g, mem_norm_g, w_mem_kv, mem_q_norm_g,
           mem_k_norm_g, w_o_da, w_o_ret, w_o_mem, w_out, ffn_norm_g, w_ffn_gate, w_ffn_up, w_ffn_down):
    lambda_init = 0.8 - 0.6 * math.exp(-0.3 * l)
    row = lambda a: a.astype(F32).reshape(1, -1)
    w_in_rot = jnp.concatenate([w_in[:, W_IN_GATE_OFF:], w_in[:, :W_IN_GATE_OFF]], axis=1).astype(BF16)
    proj = _in_proj(x, row(attn_norm_g), w_in_rot)
    da_qt, da_k, da_vt, rq, rkt = _qk_prep(proj, positions, da_q_norm_g, da_k_norm_g)
    lam_params = jnp.stack([da_lambda_q1, da_lambda_k1, da_lambda_q2, da_lambda_k2]).astype(F32)
    o_da = _diff_attn(da_qt, da_k, da_vt, lam_params, row(da_subln_g), lambda_init)
    o_ret = _retention(rq, rkt, proj, row(ret_norm_g))
    kv_mem = _mem_kv(mem, row(mem_norm_g), w_mem_kv.astype(BF16), row(mem_k_norm_g))
    x1 = _merge(x, o_da, o_ret, proj, kv_mem, row(mem_q_norm_g), w_o_da.astype(BF16),
                w_o_ret.astype(BF16), w_o_mem.astype(BF16), w_out.astype(BF16))
    return _ffn(x1, row(ffn_norm_g), w_ffn_gate.astype(BF16), w_ffn_up.astype(BF16),
                w_ffn_down.astype(BF16))


def kernel(x, mem, positions, attn_norm_g, w_in, da_q_norm_g, da_k_norm_g, da_lambda_q1, da_lambda_k1,
           da_lambda_q2, da_lambda_k2, da_subln_g, ret_norm_g, mem_norm_g, w_mem_kv, mem_q_norm_g,
           mem_k_norm_g, w_o_da, w_o_ret, w_o_mem, w_out, ffn_norm_g, w_ffn_gate, w_ffn_up, w_ffn_down):
    batch, depth = x.shape[0], w_in.shape[0]
    outs = []
    for b in range(batch):
        xb = x[b]
        for l in range(depth):
            xb = _layer(xb, mem[b], positions[b], l, attn_norm_g[l], w_in[l], da_q_norm_g[l], da_k_norm_g[l],
                        da_lambda_q1[l], da_lambda_k1[l], da_lambda_q2[l], da_lambda_k2[l], da_subln_g[l],
                        ret_norm_g[l], mem_norm_g[l], w_mem_kv[l], mem_q_norm_g[l], mem_k_norm_g[l],
                        w_o_da[l], w_o_ret[l], w_o_mem[l], w_out[l], ffn_norm_g[l], w_ffn_gate[l],
                        w_ffn_up[l], w_ffn_down[l])
        outs.append(xb)
    return jnp.stack(outs)
```

```python
import functools
import math

import jax
import jax.numpy as jnp
from jax import lax
from jax.experimental import pallas as pl
from jax.experimental.pallas import tpu as pltpu

F32 = jnp.float32
BF16 = jnp.bfloat16

D_MODEL = 2048
EPS = 1e-6
LOG2E = math.log2(math.e)

DA_HEADS = 8
DA_QK_DIM = 64
DA_V_DIM = 128
DA_ROT_DIM = 16
ROPE_THETA = 500000.0

RET_HEADS = 8
RET_QK_DIM = 64
RET_V_DIM = 128
RET_ROT_BASE = 10000.0

MEM_HEADS = 4
MEM_HEAD_DIM = 256

D_FF = 5632

W_IN_GATE_OFF = 7168
GATE_OFF = 0
DA_Q_OFF = 6144
DA_K_OFF = 7168
DA_V_OFF = 8192
RET_Q_OFF = 9216
RET_K_OFF = 9728
RET_V_OFF = 10240
RET_G_OFF = 11264
MEM_Q_OFF = 12288
IN_COLS = 13312

LANES = 128
BF16_SUBLANES = 16
MXU_DIM = 256
VMEM_LIMIT = 60 * 1024 * 1024
NEG_BIG = -1e30
DA_VT_ROWS = DA_V_DIM + BF16_SUBLANES

IN_TM, IN_TN = 1024, 1024
PREP_TM = 512
DA_TQ, DA_TK = 512, 512
RET_CHUNK = 256
MERGE_TM = 256
FFN_TM, FFN_TF = 1024, 512


def _params(*sem):
    return pltpu.CompilerParams(dimension_semantics=sem, vmem_limit_bytes=VMEM_LIMIT)


def _rms(x, g):
    return x * lax.rsqrt(jnp.mean(x * x, axis=-1, keepdims=True) + EPS) * g


def _in_proj_body(x_ref, g_ref, w_ref, o_ref, h_ref):
    @pl.when(pl.program_id(1) == 0)
    def _():
        h_ref[...] = _rms(x_ref[...], g_ref[...]).astype(BF16)

    o_ref[...] = jnp.dot(h_ref[...], w_ref[...], preferred_element_type=F32).astype(o_ref.dtype)


def _in_proj(x, g, w):
    s = x.shape[0]
    return pl.pallas_call(
        _in_proj_body,
        out_shape=jax.ShapeDtypeStruct((s, IN_COLS), BF16),
        grid=(s // IN_TM, IN_COLS // IN_TN),
        in_specs=[
            pl.BlockSpec((IN_TM, D_MODEL), lambda i, j: (i, 0)),
            pl.BlockSpec((1, D_MODEL), lambda i, j: (0, 0)),
            pl.BlockSpec((D_MODEL, IN_TN), lambda i, j: (0, (j + W_IN_GATE_OFF // IN_TN) % (IN_COLS // IN_TN))),
        ],
        out_specs=pl.BlockSpec((IN_TM, IN_TN), lambda i, j: (i, j)),
        scratch_shapes=[pltpu.VMEM((IN_TM, D_MODEL), BF16)],
        compiler_params=_params("arbitrary", "arbitrary"),
        name="in_proj",
    )(x, g, w)


def _rotate(x, cos, sin_signed, first_half, half):
    w = x.shape[-1]
    partner = jnp.where(first_half, pltpu.roll(x, w - half, 1), pltpu.roll(x, half, 1))
    return x * cos + partner * sin_signed


def _qk_prep_body(pos_ref, invf_ref, dqk_ref, v_ref, rqk_ref, gain_ref, ones_ref,
                  qt_out, k_out, vt_out, rq_out, rkt_out):
    tm = pos_ref.shape[0]
    cw = MXU_DIM
    ang = pos_ref[...].astype(F32) * invf_ref[...]
    c, s = jnp.cos(ang), jnp.sin(ang)
    lane = lax.broadcasted_iota(jnp.int32, (tm, LANES), 1)
    low = lane < 64
    c_sw, s_sw = pltpu.roll(c, 64, 1), pltpu.roll(s, 64, 1)
    l64 = lane % 64
    ret_first = l64 < RET_QK_DIM // 2
    da_first = l64 < DA_ROT_DIM // 2
    c_r = jnp.where(low, c, c_sw)
    s_r = jnp.where(low, s, s_sw)
    s_r = jnp.where(ret_first, -s_r, s_r)
    c_d = jnp.where(low, c_sw, c)
    s_d = jnp.where(low, s_sw, s)
    s_d = jnp.where(da_first, -s_d, s_d)

    def wide(t):
        return jnp.concatenate([t, t], axis=1)

    c_r, s_r, c_d, s_d = wide(c_r), wide(s_r), wide(c_d), wide(s_d)
    ret_first, da_first = wide(ret_first), wide(da_first)

    ones_bd = ones_ref[...]
    n_q = (DA_K_OFF - DA_Q_OFF) // cw
    for ch in range(dqk_ref.shape[1] // cw):
        sl = slice(ch * cw, (ch + 1) * cw)
        x = dqk_ref[:, sl].astype(F32)
        ss = jnp.dot((x * x).astype(BF16), ones_bd, preferred_element_type=F32)
        xn = x * lax.rsqrt(ss * (1.0 / DA_QK_DIM) + EPS) * gain_ref[:, sl]
        y = _rotate(xn, c_d, s_d, da_first, DA_ROT_DIM // 2)
        if ch < n_q:
            qt_out[sl, :] = (y * (DA_QK_DIM ** -0.5 * LOG2E)).T.astype(BF16)
        else:
            k_out[:, (ch - n_q) * cw:(ch - n_q + 1) * cw] = y.astype(BF16)
    for h in range(DA_HEADS):
        vt_out[h, 0:DA_V_DIM, :] = v_ref[:, h * DA_V_DIM:(h + 1) * DA_V_DIM].astype(F32).T.astype(BF16)
        vt_out[h, DA_V_DIM:DA_VT_ROWS, :] = jnp.ones((DA_VT_ROWS - DA_V_DIM, tm), BF16)
    for ch in range(rqk_ref.shape[1] // cw):
        sl = slice(ch * cw, (ch + 1) * cw)
        x = rqk_ref[:, sl].astype(F32)
        y = _rotate(x, c_r, s_r, ret_first, RET_QK_DIM // 2)
        if ch * cw >= RET_K_OFF - RET_Q_OFF:
            y = y * (RET_QK_DIM ** -0.5)
        is_k = ch * cw >= RET_K_OFF - RET_Q_OFF
        for half in range(cw // LANES):
            yp = y[:, half * LANES:(half + 1) * LANES]
            slots = (jnp.where(low, yp, 0.0), jnp.where(low, pltpu.roll(yp, 64, 1), 0.0))
            for sub, piece in enumerate(slots):
                base = ((ch * cw // LANES + half) * 2 + sub) * LANES
                if is_k:
                    base -= RET_HEADS * LANES
                    rkt_out[base:base + LANES, :] = piece.T.astype(BF16)
                else:
                    rq_out[:, base:base + LANES] = piece.astype(BF16)


def _qk_prep(proj, positions, da_q_g, da_k_g):
    s = proj.shape[0]
    tm = PREP_TM
    half_r = RET_QK_DIM // 2
    half_d = DA_ROT_DIM // 2
    inv_r = RET_ROT_BASE ** (-jnp.arange(half_r, dtype=F32) * 2.0 / RET_QK_DIM)
    inv_d = ROPE_THETA ** (-jnp.arange(half_d, dtype=F32) * 2.0 / DA_ROT_DIM)
    inv = jnp.concatenate([inv_r, inv_r, inv_d, inv_d, jnp.zeros((64 - DA_ROT_DIM,), F32)]).reshape(1, LANES)
    gain = jnp.concatenate([jnp.tile(da_q_g.astype(F32), 2 * DA_HEADS),
                            jnp.tile(da_k_g.astype(F32), 2 * DA_HEADS)]).reshape(1, 2048)
    blk = jnp.arange(MXU_DIM) // DA_QK_DIM
    ones_bd = (blk[:, None] == blk[None, :]).astype(BF16)
    return pl.pallas_call(
        _qk_prep_body,
        out_shape=(jax.ShapeDtypeStruct((1024, s), BF16),
                   jax.ShapeDtypeStruct((s, 1024), BF16),
                   jax.ShapeDtypeStruct((DA_HEADS, DA_VT_ROWS, s), BF16),
                   jax.ShapeDtypeStruct((s, RET_HEADS * LANES), BF16),
                   jax.ShapeDtypeStruct((RET_HEADS * LANES, s), BF16)),
        grid=(s // tm,),
        in_specs=[
            pl.BlockSpec((tm, 1), lambda i: (i, 0)),
            pl.BlockSpec((1, LANES), lambda i: (0, 0)),
            pl.BlockSpec((tm, 2048), lambda i: (i, DA_Q_OFF // 2048)),
            pl.BlockSpec((tm, 1024), lambda i: (i, DA_V_OFF // 1024)),
            pl.BlockSpec((tm, 1024), lambda i: (i, RET_Q_OFF // 1024)),
            pl.BlockSpec((1, 2048), lambda i: (0, 0)),
            pl.BlockSpec((MXU_DIM, MXU_DIM), lambda i: (0, 0)),
        ],
        out_specs=(pl.BlockSpec((1024, tm), lambda i: (0, i)),
                   pl.BlockSpec((tm, 1024), lambda i: (i, 0)),
                   pl.BlockSpec((DA_HEADS, DA_VT_ROWS, tm), lambda i: (0, 0, i)),
                   pl.BlockSpec((tm, RET_HEADS * LANES), lambda i: (i, 0)),
                   pl.BlockSpec((RET_HEADS * LANES, tm), lambda i: (0, i))),
        compiler_params=_params("arbitrary"),
        name="qk_prep",
    )(positions.reshape(s, 1), inv, proj, proj, proj, gain, ones_bd)


def _diff_attn_body(lam_ref, g_ref, qt_ref, k_ref, vt_ref, o_ref, sa_ref, sb_ref, mxa_ref, mxb_ref,
                    m_ref, acc_ref, *, lambda_init):
    tq = qt_ref.shape[1]
    tk = sa_ref.shape[1]
    ratio = tq // tk
    qi = pl.program_id(1)
    last = k_ref.shape[0] // tk - 1
    row = lax.broadcasted_iota(jnp.int32, (LANES, tq), 0)
    qf = qt_ref[...].astype(F32)
    qz = (jnp.where(row < DA_QK_DIM, qf, 0.0).astype(BF16),
          jnp.where(row >= DA_QK_DIM, qf, 0.0).astype(BF16))
    m_ref[...] = jnp.full(m_ref.shape, NEG_BIG, F32)
    acc_ref[...] = jnp.zeros(acc_ref.shape, F32)
    rel = (lax.broadcasted_iota(jnp.int32, (tk, tq), 0) - lax.broadcasted_iota(jnp.int32, (tk, tq), 1))

    def scores(j, s_ref, mx_ref, masked):
        off = pl.multiple_of(jnp.minimum(j, last) * tk, tk)
        ks = k_ref[pl.ds(off, tk), :]
        for mp in range(2):
            sc = jnp.dot(ks, qz[mp], preferred_element_type=F32)
            if masked:
                sc = jnp.where(rel <= qi * tq - j * tk, sc, NEG_BIG)
            s_ref[mp] = sc
            mx_ref[mp] = jnp.max(sc, axis=0, keepdims=True)

    def update(j, s_ref, mx_ref):
        off = pl.multiple_of(jnp.minimum(j, last) * tk, tk)
        vts = vt_ref[:, pl.ds(off, tk)]
        for mp in range(2):
            m_old = m_ref[mp]
            m_new = jnp.maximum(m_old, mx_ref[mp])
            alpha = jnp.exp2(m_old - m_new)
            p = jnp.exp2(s_ref[mp] - m_new).astype(BF16)
            acc_ref[mp] = alpha * acc_ref[mp] + jnp.dot(vts, p, preferred_element_type=F32)
            m_ref[mp] = m_new

    scores(0, sa_ref, mxa_ref, True)
    n_pairs = jnp.maximum(qi * ratio - 1, 0) // 2

    def pair(t, carry):
        scores(2 * t + 1, sb_ref, mxb_ref, False)
        update(2 * t, sa_ref, mxa_ref)
        scores(2 * t + 2, sa_ref, mxa_ref, False)
        update(2 * t + 1, sb_ref, mxb_ref)
        return carry

    lax.fori_loop(0, n_pairs, pair, 0)
    e0 = 2 * n_pairs
    bufs = ((sa_ref, mxa_ref), (sb_ref, mxb_ref))
    for r in range(1, ratio + 1):
        scores(e0 + r, *bufs[r % 2], True)
        update(e0 + r - 1, *bufs[(r - 1) % 2])
    has_extra = (qi + 1) * ratio - 1 - e0 > ratio

    @pl.when(has_extra)
    def _():
        scores(e0 + ratio + 1, *bufs[(ratio + 1) % 2], True)
        update(e0 + ratio, *bufs[ratio % 2])
        update(e0 + ratio + 1, *bufs[(ratio + 1) % 2])

    @pl.when(jnp.logical_not(has_extra))
    def _():
        update(e0 + ratio, *bufs[ratio % 2])

    lq1, lk1 = lam_ref[0:1, :], lam_ref[1:2, :]
    lq2, lk2 = lam_ref[2:3, :], lam_ref[3:4, :]
    lam = (jnp.exp(jnp.sum(lq1 * lk1, axis=-1, keepdims=True))
           - jnp.exp(jnp.sum(lq2 * lk2, axis=-1, keepdims=True)) + lambda_init)
    a1, a2 = acc_ref[0], acc_ref[1]
    dv = DA_V_DIM
    ot = a1[0:dv] / a1[dv:dv + 1] - lam * (a2[0:dv] / a2[dv:dv + 1])
    ot = ot * lax.rsqrt(jnp.mean(ot * ot, axis=0, keepdims=True) + EPS)
    o_ref[...] = (ot.T * g_ref[...] * (1.0 - lambda_init)).astype(o_ref.dtype)


def _diff_attn(qt, k, vt, lam_params, subln_g, lambda_init):
    s = k.shape[0]
    tq = DA_TQ
    body = functools.partial(_diff_attn_body, lambda_init=lambda_init)
    return pl.pallas_call(
        body,
        out_shape=jax.ShapeDtypeStruct((s, DA_HEADS * DA_V_DIM), BF16),
        grid=(DA_HEADS, s // tq),
        in_specs=[
            pl.BlockSpec((4, DA_QK_DIM), lambda h, i: (0, 0)),
            pl.BlockSpec((1, DA_V_DIM), lambda h, i: (0, 0)),
            pl.BlockSpec((LANES, tq), lambda h, i: (h, i)),
            pl.BlockSpec((s, LANES), lambda h, i: (0, h)),
            pl.BlockSpec((None, DA_VT_ROWS, s), lambda h, i: (h, 0, 0)),
        ],
        out_specs=pl.BlockSpec((tq, DA_V_DIM), lambda h, i: (i, h)),
        scratch_shapes=[pltpu.VMEM((2, DA_TK, tq), F32), pltpu.VMEM((2, DA_TK, tq), F32),
                        pltpu.VMEM((2, 1, tq), F32), pltpu.VMEM((2, 1, tq), F32),
                        pltpu.VMEM((2, 1, tq), F32), pltpu.VMEM((2, DA_VT_ROWS, tq), F32)],
        compiler_params=_params("arbitrary", "arbitrary"),
        name="diff_attn",
    )(lam_params, subln_g, qt, k, vt)


def _retention_body(q_ref, k_ref, v_ref, gate_ref, g_ref, o_ref, state_ref, decay_ref, qdec_ref, kdec_ref):
    c = q_ref.shape[0]
    log_gamma = [math.log(1.0 - 2.0 ** (-5.0 - h)) for h in range(RET_HEADS)]

    @pl.when(pl.program_id(0) == 0)
    def _():
        state_ref[...] = jnp.zeros(state_ref.shape, F32)
        row = lax.broadcasted_iota(jnp.int32, (c, c), 0)
        col = lax.broadcasted_iota(jnp.int32, (c, c), 1)
        rel = (row - col).astype(F32)
        idx = lax.broadcasted_iota(jnp.int32, (c, 1), 0).astype(F32)
        idx_row = lax.broadcasted_iota(jnp.int32, (1, c), 1).astype(F32)
        for h in range(RET_HEADS):
            decay_ref[h] = jnp.where(rel >= 0, jnp.exp(log_gamma[h] * jnp.maximum(rel, 0.0)), 0.0)
            qdec_ref[h] = jnp.exp(log_gamma[h] * (idx + 1.0))
            kdec_ref[h] = jnp.exp(log_gamma[h] * (c - 1.0 - idx_row))

    for h in range(RET_HEADS):
        qb = q_ref[:, h * LANES:(h + 1) * LANES]
        kt = k_ref[h * LANES:(h + 1) * LANES, :]
        vh = v_ref[:, h * RET_V_DIM:(h + 1) * RET_V_DIM]
        scores = jnp.dot(qb, kt, preferred_element_type=F32) * decay_ref[h]
        o_in = jnp.dot(scores.astype(BF16), vh, preferred_element_type=F32)
        state = state_ref[h]
        o_cross = jnp.dot(qb, state.astype(BF16), preferred_element_type=F32) * qdec_ref[h]
        kd = (kt.astype(F32) * kdec_ref[h]).astype(BF16)
        state_ref[h] = state * math.exp(log_gamma[h] * c) + jnp.dot(kd, vh, preferred_element_type=F32)
        o = _rms(o_in + o_cross, g_ref[...])
        gate = gate_ref[:, h * RET_V_DIM:(h + 1) * RET_V_DIM].astype(F32)
        o_ref[:, h * RET_V_DIM:(h + 1) * RET_V_DIM] = (gate * jax.nn.sigmoid(gate) * o).astype(o_ref.dtype)


def _retention(rq, rkt, proj, norm_g):
    s = rq.shape[0]
    c = RET_CHUNK
    w = RET_HEADS * RET_V_DIM
    return pl.pallas_call(
        _retention_body,
        out_shape=jax.ShapeDtypeStruct((s, w), BF16),
        grid=(s // c,),
        in_specs=[
            pl.BlockSpec((c, RET_HEADS * LANES), lambda i: (i, 0)),
            pl.BlockSpec((RET_HEADS * LANES, c), lambda i: (0, i)),
            pl.BlockSpec((c, w), lambda i: (i, RET_V_OFF // w)),
            pl.BlockSpec((c, w), lambda i: (i, RET_G_OFF // w)),
            pl.BlockSpec((1, RET_V_DIM), lambda i: (0, 0)),
        ],
        out_specs=pl.BlockSpec((c, w), lambda i: (i, 0)),
        scratch_shapes=[pltpu.VMEM((RET_HEADS, LANES, RET_V_DIM), F32),
                        pltpu.VMEM((RET_HEADS, c, c), F32),
                        pltpu.VMEM((RET_HEADS, c, 1), F32),
                        pltpu.VMEM((RET_HEADS, 1, c), F32)],
        compiler_params=_params("arbitrary"),
        name="retention",
    )(rq, rkt, proj, proj, norm_g)


def _mem_kv_body(mem_ref, g_ref, w_ref, kg_ref, o_ref):
    m = _rms(mem_ref[...], g_ref[...]).astype(BF16)
    kv = jnp.dot(m, w_ref[...].astype(BF16), preferred_element_type=F32)
    kn = _rms(kv, kg_ref[...]) * (MEM_HEAD_DIM ** -0.5)
    o_ref[...] = jnp.where(pl.program_id(0) < MEM_HEADS, kn, kv).astype(o_ref.dtype)


def _mem_kv(mem, mem_norm_g, w_mem_kv, k_g):
    n = mem.shape[0]
    hd = MEM_HEAD_DIM
    return pl.pallas_call(
        _mem_kv_body,
        out_shape=jax.ShapeDtypeStruct((n, 2 * MEM_HEADS * hd), BF16),
        grid=(2 * MEM_HEADS,),
        in_specs=[
            pl.BlockSpec((n, D_MODEL), lambda j: (0, 0)),
            pl.BlockSpec((1, D_MODEL), lambda j: (0, 0)),
            pl.BlockSpec((D_MODEL, hd), lambda j: (0, j)),
            pl.BlockSpec((1, hd), lambda j: (0, 0)),
        ],
        out_specs=pl.BlockSpec((n, hd), lambda j: (0, j)),
        compiler_params=_params("arbitrary"),
        name="mem_kv",
    )(mem, mem_norm_g, w_mem_kv, k_g)


def _merge_body(x_ref, oda_ref, oret_ref, mq_ref, gates_ref, kv_ref, qg_ref,
                wda_ref, wret_ref, wmem_ref, wout_ref, o_ref, omem_ref):
    hd = MEM_HEAD_DIM
    for h in range(MEM_HEADS):
        q = _rms(mq_ref[:, h * hd:(h + 1) * hd].astype(F32), qg_ref[...]).astype(BF16)
        k = kv_ref[:, h * hd:(h + 1) * hd]
        v = kv_ref[:, (MEM_HEADS + h) * hd:(MEM_HEADS + h + 1) * hd]
        sc = lax.dot_general(q, k, (((1,), (1,)), ((), ())), preferred_element_type=F32)
        p = jnp.exp(sc - jnp.max(sc, axis=-1, keepdims=True))
        o = jnp.dot(p.astype(BF16), v, preferred_element_type=F32) / jnp.sum(p, axis=-1, keepdims=True)
        omem_ref[:, h * hd:(h + 1) * hd] = o.astype(BF16)

    def branch(b, act, w_ref):
        gate = gates_ref[:, b * D_MODEL:(b + 1) * D_MODEL].astype(F32)
        return jax.nn.sigmoid(gate) * jnp.dot(act, w_ref[...], preferred_element_type=F32)

    merged = (branch(0, oda_ref[...], wda_ref) + branch(1, oret_ref[...], wret_ref)
              + branch(2, omem_ref[...], wmem_ref))
    o_ref[...] = x_ref[...] + jnp.dot(merged.astype(BF16), wout_ref[...], preferred_element_type=F32)


def _merge(x, o_da, o_ret, proj, kv_mem, mem_q_g, w_o_da, w_o_ret, w_o_mem, w_out):
    s = x.shape[0]
    tm = MERGE_TM
    n_mem = kv_mem.shape[0]
    bw = 1024
    gw = 3 * D_MODEL
    row = lambda i: (i, 0)
    resident = lambda shape: pl.BlockSpec(shape, lambda i: (0, 0), pipeline_mode=pl.Buffered(1))
    return pl.pallas_call(
        _merge_body,
        out_shape=jax.ShapeDtypeStruct((s, D_MODEL), F32),
        grid=(s // tm,),
        in_specs=[
            pl.BlockSpec((tm, D_MODEL), row),
            pl.BlockSpec((tm, bw), row),
            pl.BlockSpec((tm, bw), row),
            pl.BlockSpec((tm, bw), lambda i: (i, MEM_Q_OFF // bw)),
            pl.BlockSpec((tm, gw), lambda i: (i, GATE_OFF // gw)),
            resident((n_mem, 2 * bw)),
            resident((1, MEM_HEAD_DIM)),
            resident((bw, D_MODEL)),
            resident((bw, D_MODEL)),
            resident((bw, D_MODEL)),
            resident((D_MODEL, D_MODEL)),
        ],
        out_specs=pl.BlockSpec((tm, D_MODEL), row),
        scratch_shapes=[pltpu.VMEM((tm, bw), BF16)],
        compiler_params=_params("arbitrary"),
        name="merge",
    )(x, o_da, o_ret, proj, proj, kv_mem, mem_q_g, w_o_da, w_o_ret, w_o_mem, w_out)


def _ffn_body(x_ref, g_ref, wg_ref, wu_ref, wd_ref, o_ref, h_ref):
    @pl.when(pl.program_id(1) == 0)
    def _():
        x = x_ref[...]
        o_ref[...] = x
        h_ref[...] = _rms(x, g_ref[...]).astype(BF16)

    h = h_ref[...]
    gate = jnp.dot(h, wg_ref[...], preferred_element_type=F32)
    up = jnp.dot(h, wu_ref[...], preferred_element_type=F32)
    act = (gate * jax.nn.sigmoid(gate) * up).astype(BF16)
    o_ref[...] += jnp.dot(act, wd_ref[...], preferred_element_type=F32)


def _ffn(x, g, w_gate, w_up, w_down):
    s = x.shape[0]
    tm, tf = FFN_TM, FFN_TF
    return pl.pallas_call(
        _ffn_body,
        out_shape=jax.ShapeDtypeStruct((s, D_MODEL), F32),
        grid=(s // tm, D_FF // tf),
        in_specs=[
            pl.BlockSpec((tm, D_MODEL), lambda i, j: (i, 0)),
            pl.BlockSpec((1, D_MODEL), lambda i, j: (0, 0)),
            pl.BlockSpec((D_MODEL, tf), lambda i, j: (0, j)),
            pl.BlockSpec((D_MODEL, tf), lambda i, j: (0, j)),
            pl.BlockSpec((tf, D_MODEL), lambda i, j: (j, 0)),
        ],
        out_specs=pl.BlockSpec((tm, D_MODEL), lambda i, j: (i, 0)),
        scratch_shapes=[pltpu.VMEM((tm, D_MODEL), BF16)],
        compiler_params=_params("arbitrary", "arbitrary"),
        name="ffn",
    )(x, g, w_gate, w_up, w_down)


def _layer(x, mem, positions, l, attn_norm_g, w_in, da_q_norm_g, da_k_norm_g, da_lambda_q1, da_lambda_k1,
           da_lambda_q2, da_lambda_k2, da_subln_g, ret_norm_g, mem_norm_g, w_mem_kv, mem_q_norm_g,
           mem_k_norm_g, w_o_da, w_o_ret, w_o_mem, w_out, ffn_norm_g, w_ffn_gate, w_ffn_up, w_ffn_down):
    lambda_init = 0.8 - 0.6 * math.exp(-0.3 * l)
    row = lambda a: a.astype(F32).reshape(1, -1)
    proj = _in_proj(x, row(attn_norm_g), w_in.astype(BF16))
    da_qt, da_k, da_vt, rq, rkt = _qk_prep(proj, positions, da_q_norm_g, da_k_norm_g)
    lam_params = jnp.stack([da_lambda_q1, da_lambda_k1, da_lambda_q2, da_lambda_k2]).astype(F32)
    o_da = _diff_attn(da_qt, da_k, da_vt, lam_params, row(da_subln_g), lambda_init)
    o_ret = _retention(rq, rkt, proj, row(ret_norm_g))
    kv_mem = _mem_kv(mem, row(mem_norm_g), w_mem_kv, row(mem_k_norm_g))
    x1 = _merge(x, o_da, o_ret, proj, kv_mem, row(mem_q_norm_g), w_o_da.astype(BF16),
                w_o_ret.astype(BF16), w_o_mem.astype(BF16), w_out.astype(BF16))
    return _ffn(x1, row(ffn_norm_g), w_ffn_gate.astype(BF16), w_ffn_up.astype(BF16),
                w_ffn_down.astype(BF16))


def kernel(x, mem, positions, attn_norm_g, w_in, da_q_norm_g, da_k_norm_g, da_lambda_q1, da_lambda_k1,
           da_lambda_q2, da_lambda_k2, da_subln_g, ret_norm_g, mem_norm_g, w_mem_kv, mem_q_norm_g,
           mem_k_norm_g, w_o_da, w_o_ret, w_o_mem, w_out, ffn_norm_g, w_ffn_gate, w_ffn_up, w_ffn_down):
    batch, depth = x.shape[0], w_in.shape[0]
    outs = []
    for b in range(batch):
        xb = x[b]
        for l in range(depth):
            xb = _layer(xb, mem[b], positions[b], l, attn_norm_g[l], w_in[l], da_q_norm_g[l], da_k_norm_g[l],
                        da_lambda_q1[l], da_lambda_k1[l], da_lambda_q2[l], da_lambda_k2[l], da_subln_g[l],
                        ret_norm_g[l], mem_norm_g[l], w_mem_kv[l], mem_q_norm_g[l], mem_k_norm_g[l],
                        w_o_da[l], w_o_ret[l], w_o_mem[l], w_out[l], ffn_norm_g[l], w_ffn_gate[l],
                        w_ffn_up[l], w_ffn_down[l])
        outs.append(xb)
    return jnp.stack(outs)
```

```python
import functools
import math

import jax
import jax.numpy as jnp
from jax import lax
from jax.experimental import pallas as pl
from jax.experimental.pallas import tpu as pltpu

F32 = jnp.float32
BF16 = jnp.bfloat16

D_MODEL = 2048
EPS = 1e-6
LOG2E = math.log2(math.e)

DA_HEADS = 8
DA_QK_DIM = 64
DA_V_DIM = 128
DA_ROT_DIM = 16
ROPE_THETA = 500000.0

RET_HEADS = 8
RET_QK_DIM = 64
RET_V_DIM = 128
RET_ROT_BASE = 10000.0

MEM_HEADS = 4
MEM_HEAD_DIM = 256

D_FF = 5632

W_IN_GATE_OFF = 7168
GATE_OFF = 0
DA_Q_OFF = 6144
DA_K_OFF = 7168
DA_V_OFF = 8192
RET_Q_OFF = 9216
RET_K_OFF = 9728
RET_V_OFF = 10240
RET_G_OFF = 11264
MEM_Q_OFF = 12288
IN_COLS = 13312

LANES = 128
BF16_SUBLANES = 16
MXU_DIM = 256
VMEM_LIMIT = 60 * 1024 * 1024
NEG_BIG = -1e30
DA_VT_ROWS = DA_V_DIM + BF16_SUBLANES

IN_TM, IN_TN = 1024, 1024
PREP_TM = 512
DA_TQ, DA_TK = 512, 512
DA_HPS = 2
RET_CHUNK = 256
MERGE_TM = 256
FFN_TM, FFN_TF = 1024, 512


def _params(*sem):
    return pltpu.CompilerParams(dimension_semantics=sem, vmem_limit_bytes=VMEM_LIMIT)


def _rms(x, g):
    return x * lax.rsqrt(jnp.mean(x * x, axis=-1, keepdims=True) + EPS) * g


def _in_proj_body(x_ref, g_ref, w_ref, o_ref, h_ref):
    @pl.when(pl.program_id(1) == 0)
    def _():
        h_ref[...] = _rms(x_ref[...], g_ref[...]).astype(BF16)

    o_ref[...] = jnp.dot(h_ref[...], w_ref[...], preferred_element_type=F32).astype(o_ref.dtype)


def _in_proj(x, g, w):
    s = x.shape[0]
    return pl.pallas_call(
        _in_proj_body,
        out_shape=jax.ShapeDtypeStruct((s, IN_COLS), BF16),
        grid=(s // IN_TM, IN_COLS // IN_TN),
        in_specs=[
            pl.BlockSpec((IN_TM, D_MODEL), lambda i, j: (i, 0)),
            pl.BlockSpec((1, D_MODEL), lambda i, j: (0, 0)),
            pl.BlockSpec((D_MODEL, IN_TN), lambda i, j: (0, (j + W_IN_GATE_OFF // IN_TN) % (IN_COLS // IN_TN))),
        ],
        out_specs=pl.BlockSpec((IN_TM, IN_TN), lambda i, j: (i, j)),
        scratch_shapes=[pltpu.VMEM((IN_TM, D_MODEL), BF16)],
        compiler_params=_params("arbitrary", "arbitrary"),
        name="in_proj",
    )(x, g, w)


def _rotate(x, cos, sin_signed, first_half, half):
    w = x.shape[-1]
    partner = jnp.where(first_half, pltpu.roll(x, w - half, 1), pltpu.roll(x, half, 1))
    return x * cos + partner * sin_signed


def _qk_prep_body(pos_ref, invf_ref, dqk_ref, v_ref, rqk_ref, gain_ref, ones_ref,
                  qt_out, k_out, vt_out, rq_out, rkt_out):
    tm = pos_ref.shape[0]
    cw = MXU_DIM
    ang = pos_ref[...].astype(F32) * invf_ref[...]
    c, s = jnp.cos(ang), jnp.sin(ang)
    lane = lax.broadcasted_iota(jnp.int32, (tm, LANES), 1)
    low = lane < 64
    c_sw, s_sw = pltpu.roll(c, 64, 1), pltpu.roll(s, 64, 1)
    l64 = lane % 64
    ret_first = l64 < RET_QK_DIM // 2
    da_first = l64 < DA_ROT_DIM // 2
    c_r = jnp.where(low, c, c_sw)
    s_r = jnp.where(low, s, s_sw)
    s_r = jnp.where(ret_first, -s_r, s_r)
    c_d = jnp.where(low, c_sw, c)
    s_d = jnp.where(low, s_sw, s)
    s_d = jnp.where(da_first, -s_d, s_d)

    def wide(t):
        return jnp.concatenate([t, t], axis=1)

    c_r, s_r, c_d, s_d = wide(c_r), wide(s_r), wide(c_d), wide(s_d)
    ret_first, da_first = wide(ret_first), wide(da_first)

    ones_bd = ones_ref[...]
    n_q = (DA_K_OFF - DA_Q_OFF) // cw
    for ch in range(dqk_ref.shape[1] // cw):
        sl = slice(ch * cw, (ch + 1) * cw)
        x = dqk_ref[:, sl].astype(F32)
        ss = jnp.dot((x * x).astype(BF16), ones_bd, preferred_element_type=F32)
        xn = x * lax.rsqrt(ss * (1.0 / DA_QK_DIM) + EPS) * gain_ref[:, sl]
        y = _rotate(xn, c_d, s_d, da_first, DA_ROT_DIM // 2)
        if ch < n_q:
            qt_out[sl, :] = (y * (DA_QK_DIM ** -0.5 * LOG2E)).T.astype(BF16)
        else:
            k_out[:, (ch - n_q) * cw:(ch - n_q + 1) * cw] = y.astype(BF16)
    for h in range(DA_HEADS):
        vt_out[h, 0:DA_V_DIM, :] = v_ref[:, h * DA_V_DIM:(h + 1) * DA_V_DIM].astype(F32).T.astype(BF16)
        vt_out[h, DA_V_DIM:DA_VT_ROWS, :] = jnp.ones((DA_VT_ROWS - DA_V_DIM, tm), BF16)
    for ch in range(rqk_ref.shape[1] // cw):
        sl = slice(ch * cw, (ch + 1) * cw)
        x = rqk_ref[:, sl].astype(F32)
        y = _rotate(x, c_r, s_r, ret_first, RET_QK_DIM // 2)
        if ch * cw >= RET_K_OFF - RET_Q_OFF:
            y = y * (RET_QK_DIM ** -0.5)
        is_k = ch * cw >= RET_K_OFF - RET_Q_OFF
        for half in range(cw // LANES):
            yp = y[:, half * LANES:(half + 1) * LANES]
            slots = (jnp.where(low, yp, 0.0), jnp.where(low, pltpu.roll(yp, 64, 1), 0.0))
            for sub, piece in enumerate(slots):
                base = ((ch * cw // LANES + half) * 2 + sub) * LANES
                if is_k:
                    base -= RET_HEADS * LANES
                    rkt_out[base:base + LANES, :] = piece.T.astype(BF16)
                else:
                    rq_out[:, base:base + LANES] = piece.astype(BF16)


def _qk_prep(proj, positions, da_q_g, da_k_g):
    s = proj.shape[0]
    tm = PREP_TM
    half_r = RET_QK_DIM // 2
    half_d = DA_ROT_DIM // 2
    inv_r = RET_ROT_BASE ** (-jnp.arange(half_r, dtype=F32) * 2.0 / RET_QK_DIM)
    inv_d = ROPE_THETA ** (-jnp.arange(half_d, dtype=F32) * 2.0 / DA_ROT_DIM)
    inv = jnp.concatenate([inv_r, inv_r, inv_d, inv_d, jnp.zeros((64 - DA_ROT_DIM,), F32)]).reshape(1, LANES)
    gain = jnp.concatenate([jnp.tile(da_q_g.astype(F32), 2 * DA_HEADS),
                            jnp.tile(da_k_g.astype(F32), 2 * DA_HEADS)]).reshape(1, 2048)
    blk = jnp.arange(MXU_DIM) // DA_QK_DIM
    ones_bd = (blk[:, None] == blk[None, :]).astype(BF16)
    return pl.pallas_call(
        _qk_prep_body,
        out_shape=(jax.ShapeDtypeStruct((1024, s), BF16),
                   jax.ShapeDtypeStruct((s, 1024), BF16),
                   jax.ShapeDtypeStruct((DA_HEADS, DA_VT_ROWS, s), BF16),
                   jax.ShapeDtypeStruct((s, RET_HEADS * LANES), BF16),
                   jax.ShapeDtypeStruct((RET_HEADS * LANES, s), BF16)),
        grid=(s // tm,),
        in_specs=[
            pl.BlockSpec((tm, 1), lambda i: (i, 0)),
            pl.BlockSpec((1, LANES), lambda i: (0, 0)),
            pl.BlockSpec((tm, 2048), lambda i: (i, DA_Q_OFF // 2048)),
            pl.BlockSpec((tm, 1024), lambda i: (i, DA_V_OFF // 1024)),
            pl.BlockSpec((tm, 1024), lambda i: (i, RET_Q_OFF // 1024)),
            pl.BlockSpec((1, 2048), lambda i: (0, 0)),
            pl.BlockSpec((MXU_DIM, MXU_DIM), lambda i: (0, 0)),
        ],
        out_specs=(pl.BlockSpec((1024, tm), lambda i: (0, i)),
                   pl.BlockSpec((tm, 1024), lambda i: (i, 0)),
                   pl.BlockSpec((DA_HEADS, DA_VT_ROWS, tm), lambda i: (0, 0, i)),
                   pl.BlockSpec((tm, RET_HEADS * LANES), lambda i: (i, 0)),
                   pl.BlockSpec((RET_HEADS * LANES, tm), lambda i: (0, i))),
        compiler_params=_params("arbitrary"),
        name="qk_prep",
    )(positions.reshape(s, 1), inv, proj, proj, proj, gain, ones_bd)


def _diff_attn_body(lam_ref, g_ref, qt_ref, k_ref, vt_ref, o_ref, sa_ref, sb_ref, mxa_ref, mxb_ref,
                    m_ref, acc_ref, *, lambda_init):
    tq = qt_ref.shape[1]
    tk = sa_ref.shape[1]
    ratio = tq // tk
    qi = pl.program_id(1)
    last = k_ref.shape[0] // tk - 1
    row = lax.broadcasted_iota(jnp.int32, (LANES, tq), 0)
    qz = []
    for hh in range(DA_HPS):
        qf = qt_ref[hh * LANES:(hh + 1) * LANES, :].astype(F32)
        qz.append(jnp.where(row < DA_QK_DIM, qf, 0.0).astype(BF16))
        qz.append(jnp.where(row >= DA_QK_DIM, qf, 0.0).astype(BF16))
    m_ref[...] = jnp.full(m_ref.shape, NEG_BIG, F32)
    acc_ref[...] = jnp.zeros(acc_ref.shape, F32)
    rel = (lax.broadcasted_iota(jnp.int32, (tk, tq), 0) - lax.broadcasted_iota(jnp.int32, (tk, tq), 1))

    def scores(j, s_ref, mx_ref, masked):
        off = pl.multiple_of(jnp.minimum(j, last) * tk, tk)
        for hh in range(DA_HPS):
            ks = k_ref[pl.ds(off, tk), hh * LANES:(hh + 1) * LANES]
            for st in (2 * hh, 2 * hh + 1):
                sc = jnp.dot(ks, qz[st], preferred_element_type=F32)
                if masked:
                    sc = jnp.where(rel <= qi * tq - j * tk, sc, NEG_BIG)
                s_ref[st] = sc
                mx_ref[st] = jnp.max(sc, axis=0, keepdims=True)

    def update(j, s_ref, mx_ref):
        off = pl.multiple_of(jnp.minimum(j, last) * tk, tk)
        for hh in range(DA_HPS):
            vts = vt_ref[hh, :, pl.ds(off, tk)]
            for st in (2 * hh, 2 * hh + 1):
                m_old = m_ref[st]
                m_new = jnp.maximum(m_old, mx_ref[st])
                alpha = jnp.exp2(m_old - m_new)
                p = jnp.exp2(s_ref[st] - m_new).astype(BF16)
                acc_ref[st] = alpha * acc_ref[st] + jnp.dot(vts, p, preferred_element_type=F32)
                m_ref[st] = m_new

    scores(0, sa_ref, mxa_ref, True)
    n_pairs = jnp.maximum(qi * ratio - 1, 0) // 2

    def pair(t, carry):
        scores(2 * t + 1, sb_ref, mxb_ref, False)
        update(2 * t, sa_ref, mxa_ref)
        scores(2 * t + 2, sa_ref, mxa_ref, False)
        update(2 * t + 1, sb_ref, mxb_ref)
        return carry

    lax.fori_loop(0, n_pairs, pair, 0)
    e0 = 2 * n_pairs
    bufs = ((sa_ref, mxa_ref), (sb_ref, mxb_ref))
    for r in range(1, ratio + 1):
        scores(e0 + r, *bufs[r % 2], True)
        update(e0 + r - 1, *bufs[(r - 1) % 2])
    has_extra = (qi + 1) * ratio - 1 - e0 > ratio

    @pl.when(has_extra)
    def _():
        scores(e0 + ratio + 1, *bufs[(ratio + 1) % 2], True)
        update(e0 + ratio, *bufs[ratio % 2])
        update(e0 + ratio + 1, *bufs[(ratio + 1) % 2])

    @pl.when(jnp.logical_not(has_extra))
    def _():
        update(e0 + ratio, *bufs[ratio % 2])

    lq1, lk1 = lam_ref[0:1, :], lam_ref[1:2, :]
    lq2, lk2 = lam_ref[2:3, :], lam_ref[3:4, :]
    lam = (jnp.exp(jnp.sum(lq1 * lk1, axis=-1, keepdims=True))
           - jnp.exp(jnp.sum(lq2 * lk2, axis=-1, keepdims=True)) + lambda_init)
    dv = DA_V_DIM
    for hh in range(DA_HPS):
        a1, a2 = acc_ref[2 * hh], acc_ref[2 * hh + 1]
        ot = a1[0:dv] / a1[dv:dv + 1] - lam * (a2[0:dv] / a2[dv:dv + 1])
        ot = ot * lax.rsqrt(jnp.mean(ot * ot, axis=0, keepdims=True) + EPS)
        o_ref[:, hh * dv:(hh + 1) * dv] = (ot.T * g_ref[...] * (1.0 - lambda_init)).astype(o_ref.dtype)


def _diff_attn(qt, k, vt, lam_params, subln_g, lambda_init):
    s = k.shape[0]
    tq = DA_TQ
    hps = DA_HPS
    n_st = 2 * hps
    body = functools.partial(_diff_attn_body, lambda_init=lambda_init)
    return pl.pallas_call(
        body,
        out_shape=jax.ShapeDtypeStruct((s, DA_HEADS * DA_V_DIM), BF16),
        grid=(DA_HEADS // hps, s // tq),
        in_specs=[
            pl.BlockSpec((4, DA_QK_DIM), lambda h, i: (0, 0)),
            pl.BlockSpec((1, DA_V_DIM), lambda h, i: (0, 0)),
            pl.BlockSpec((hps * LANES, tq), lambda h, i: (h, i)),
            pl.BlockSpec((s, hps * LANES), lambda h, i: (0, h)),
            pl.BlockSpec((hps, DA_VT_ROWS, s), lambda h, i: (h, 0, 0)),
        ],
        out_specs=pl.BlockSpec((tq, hps * DA_V_DIM), lambda h, i: (i, h)),
        scratch_shapes=[pltpu.VMEM((n_st, DA_TK, tq), F32), pltpu.VMEM((n_st, DA_TK, tq), F32),
                        pltpu.VMEM((n_st, 1, tq), F32), pltpu.VMEM((n_st, 1, tq), F32),
                        pltpu.VMEM((n_st, 1, tq), F32), pltpu.VMEM((n_st, DA_VT_ROWS, tq), F32)],
        compiler_params=_params("arbitrary", "arbitrary"),
        name="diff_attn",
    )(lam_params, subln_g, qt, k, vt)


def _retention_body(q_ref, k_ref, v_ref, gate_ref, g_ref, o_ref, state_ref, decay_ref, qdec_ref, kdec_ref):
    c = q_ref.shape[0]
    log_gamma = [math.log(1.0 - 2.0 ** (-5.0 - h)) for h in range(RET_HEADS)]

    @pl.when(pl.program_id(0) == 0)
    def _():
        state_ref[...] = jnp.zeros(state_ref.shape, F32)
        row = lax.broadcasted_iota(jnp.int32, (c, c), 0)
        col = lax.broadcasted_iota(jnp.int32, (c, c), 1)
        rel = (row - col).astype(F32)
        idx = lax.broadcasted_iota(jnp.int32, (c, 1), 0).astype(F32)
        idx_row = lax.broadcasted_iota(jnp.int32, (1, c), 1).astype(F32)
        for h in range(RET_HEADS):
            decay_ref[h] = jnp.where(rel >= 0, jnp.exp(log_gamma[h] * jnp.maximum(rel, 0.0)), 0.0)
            qdec_ref[h] = jnp.exp(log_gamma[h] * (idx + 1.0))
            kdec_ref[h] = jnp.exp(log_gamma[h] * (c - 1.0 - idx_row))

    for h in range(RET_HEADS):
        qb = q_ref[:, h * LANES:(h + 1) * LANES]
        kt = k_ref[h * LANES:(h + 1) * LANES, :]
        vh = v_ref[:, h * RET_V_DIM:(h + 1) * RET_V_DIM]
        scores = jnp.dot(qb, kt, preferred_element_type=F32) * decay_ref[h]
        o_in = jnp.dot(scores.astype(BF16), vh, preferred_element_type=F32)
        state = state_ref[h]
        o_cross = jnp.dot(qb, state.astype(BF16), preferred_element_type=F32) * qdec_ref[h]
        kd = (kt.astype(F32) * kdec_ref[h]).astype(BF16)
        state_ref[h] = state * math.exp(log_gamma[h] * c) + jnp.dot(kd, vh, preferred_element_type=F32)
        o = _rms(o_in + o_cross, g_ref[...])
        gate = gate_ref[:, h * RET_V_DIM:(h + 1) * RET_V_DIM].astype(F32)
        o_ref[:, h * RET_V_DIM:(h + 1) * RET_V_DIM] = (gate * jax.nn.sigmoid(gate) * o).astype(o_ref.dtype)


def _retention(rq, rkt, proj, norm_g):
    s = rq.shape[0]
    c = RET_CHUNK
    w = RET_HEADS * RET_V_DIM
    return pl.pallas_call(
        _retention_body,
        out_shape=jax.ShapeDtypeStruct((s, w), BF16),
        grid=(s // c,),
        in_specs=[
            pl.BlockSpec((c, RET_HEADS * LANES), lambda i: (i, 0)),
            pl.BlockSpec((RET_HEADS * LANES, c), lambda i: (0, i)),
            pl.BlockSpec((c, w), lambda i: (i, RET_V_OFF // w)),
            pl.BlockSpec((c, w), lambda i: (i, RET_G_OFF // w)),
            pl.BlockSpec((1, RET_V_DIM), lambda i: (0, 0)),
        ],
        out_specs=pl.BlockSpec((c, w), lambda i: (i, 0)),
        scratch_shapes=[pltpu.VMEM((RET_HEADS, LANES, RET_V_DIM), F32),
                        pltpu.VMEM((RET_HEADS, c, c), F32),
                        pltpu.VMEM((RET_HEADS, c, 1), F32),
                        pltpu.VMEM((RET_HEADS, 1, c), F32)],
        compiler_params=_params("arbitrary"),
        name="retention",
    )(rq, rkt, proj, proj, norm_g)


def _mem_kv_body(mem_ref, g_ref, w_ref, kg_ref, o_ref):
    m = _rms(mem_ref[...], g_ref[...]).astype(BF16)
    kv = jnp.dot(m, w_ref[...].astype(BF16), preferred_element_type=F32)
    kn = _rms(kv, kg_ref[...]) * (MEM_HEAD_DIM ** -0.5)
    o_ref[...] = jnp.where(pl.program_id(0) < MEM_HEADS, kn, kv).astype(o_ref.dtype)


def _mem_kv(mem, mem_norm_g, w_mem_kv, k_g):
    n = mem.shape[0]
    hd = MEM_HEAD_DIM
    return pl.pallas_call(
        _mem_kv_body,
        out_shape=jax.ShapeDtypeStruct((n, 2 * MEM_HEADS * hd), BF16),
        grid=(2 * MEM_HEADS,),
        in_specs=[
            pl.BlockSpec((n, D_MODEL), lambda j: (0, 0)),
            pl.BlockSpec((1, D_MODEL), lambda j: (0, 0)),
            pl.BlockSpec((D_MODEL, hd), lambda j: (0, j)),
            pl.BlockSpec((1, hd), lambda j: (0, 0)),
        ],
        out_specs=pl.BlockSpec((n, hd), lambda j: (0, j)),
        compiler_params=_params("arbitrary"),
        name="mem_kv",
    )(mem, mem_norm_g, w_mem_kv, k_g)


def _merge_body(x_ref, oda_ref, oret_ref, mq_ref, gates_ref, kv_ref, qg_ref,
                wda_ref, wret_ref, wmem_ref, wout_ref, o_ref, omem_ref):
    hd = MEM_HEAD_DIM
    for h in range(MEM_HEADS):
        q = _rms(mq_ref[:, h * hd:(h + 1) * hd].astype(F32), qg_ref[...]).astype(BF16)
        k = kv_ref[:, h * hd:(h + 1) * hd]
        v = kv_ref[:, (MEM_HEADS + h) * hd:(MEM_HEADS + h + 1) * hd]
        sc = lax.dot_general(q, k, (((1,), (1,)), ((), ())), preferred_element_type=F32)
        p = jnp.exp(sc - jnp.max(sc, axis=-1, keepdims=True))
        o = jnp.dot(p.astype(BF16), v, preferred_element_type=F32) / jnp.sum(p, axis=-1, keepdims=True)
        omem_ref[:, h * hd:(h + 1) * hd] = o.astype(BF16)

    def branch(b, act, w_ref):
        gate = gates_ref[:, b * D_MODEL:(b + 1) * D_MODEL].astype(F32)
        return jax.nn.sigmoid(gate) * jnp.dot(act, w_ref[...], preferred_element_type=F32)

    merged = (branch(0, oda_ref[...], wda_ref) + branch(1, oret_ref[...], wret_ref)
              + branch(2, omem_ref[...], wmem_ref))
    o_ref[...] = x_ref[...] + jnp.dot(merged.astype(BF16), wout_ref[...], preferred_element_type=F32)


def _merge(x, o_da, o_ret, proj, kv_mem, mem_q_g, w_o_da, w_o_ret, w_o_mem, w_out):
    s = x.shape[0]
    tm = MERGE_TM
    n_mem = kv_mem.shape[0]
    bw = 1024
    gw = 3 * D_MODEL
    row = lambda i: (i, 0)
    resident = lambda shape: pl.BlockSpec(shape, lambda i: (0, 0), pipeline_mode=pl.Buffered(1))
    return pl.pallas_call(
        _merge_body,
        out_shape=jax.ShapeDtypeStruct((s, D_MODEL), F32),
        grid=(s // tm,),
        in_specs=[
            pl.BlockSpec((tm, D_MODEL), row),
            pl.BlockSpec((tm, bw), row),
            pl.BlockSpec((tm, bw), row),
            pl.BlockSpec((tm, bw), lambda i: (i, MEM_Q_OFF // bw)),
            pl.BlockSpec((tm, gw), lambda i: (i, GATE_OFF // gw)),
            resident((n_mem, 2 * bw)),
            resident((1, MEM_HEAD_DIM)),
            resident((bw, D_MODEL)),
            resident((bw, D_MODEL)),
            resident((bw, D_MODEL)),
            resident((D_MODEL, D_MODEL)),
        ],
        out_specs=pl.BlockSpec((tm, D_MODEL), row),
        scratch_shapes=[pltpu.VMEM((tm, bw), BF16)],
        compiler_params=_params("arbitrary"),
        name="merge",
    )(x, o_da, o_ret, proj, proj, kv_mem, mem_q_g, w_o_da, w_o_ret, w_o_mem, w_out)


def _ffn_body(x_ref, g_ref, wg_ref, wu_ref, wd_ref, o_ref, h_ref):
    @pl.when(pl.program_id(1) == 0)
    def _():
        x = x_ref[...]
        o_ref[...] = x
        h_ref[...] = _rms(x, g_ref[...]).astype(BF16)

    h = h_ref[...]
    gate = jnp.dot(h, wg_ref[...], preferred_element_type=F32)
    up = jnp.dot(h, wu_ref[...], preferred_element_type=F32)
    act = (gate * jax.nn.sigmoid(gate) * up).astype(BF16)
    o_ref[...] += jnp.dot(act, wd_ref[...], preferred_element_type=F32)


def _ffn(x, g, w_gate, w_up, w_down):
    s = x.shape[0]
    tm, tf = FFN_TM, FFN_TF
    return pl.pallas_call(
        _ffn_body,
        out_shape=jax.ShapeDtypeStruct((s, D_MODEL), F32),
        grid=(s // tm, D_FF // tf),
        in_specs=[
            pl.BlockSpec((tm, D_MODEL), lambda i, j: (i, 0)),
            pl.BlockSpec((1, D_MODEL), lambda i, j: (0, 0)),
            pl.BlockSpec((D_MODEL, tf), lambda i, j: (0, j)),
            pl.BlockSpec((D_MODEL, tf), lambda i, j: (0, j)),
            pl.BlockSpec((tf, D_MODEL), lambda i, j: (j, 0)),
        ],
        out_specs=pl.BlockSpec((tm, D_MODEL), lambda i, j: (i, 0)),
        scratch_shapes=[pltpu.VMEM((tm, D_MODEL), BF16)],
        compiler_params=_params("arbitrary", "arbitrary"),
        name="ffn",
    )(x, g, w_gate, w_up, w_down)


def _layer(x, mem, positions, l, attn_norm_g, w_in, da_q_norm_g, da_k_norm_g, da_lambda_q1, da_lambda_k1,
           da_lambda_q2, da_lambda_k2, da_subln_g, ret_norm_g, mem_norm_g, w_mem_kv, mem_q_norm_g,
           mem_k_norm_g, w_o_da, w_o_ret, w_o_mem, w_out, ffn_norm_g, w_ffn_gate, w_ffn_up, w_ffn_down):
    lambda_init = 0.8 - 0.6 * math.exp(-0.3 * l)
    row = lambda a: a.astype(F32).reshape(1, -1)
    proj = _in_proj(x, row(attn_norm_g), w_in.astype(BF16))
    da_qt, da_k, da_vt, rq, rkt = _qk_prep(proj, positions, da_q_norm_g, da_k_norm_g)
    lam_params = jnp.stack([da_lambda_q1, da_lambda_k1, da_lambda_q2, da_lambda_k2]).astype(F32)
    o_da = _diff_attn(da_qt, da_k, da_vt, lam_params, row(da_subln_g), lambda_init)
    o_ret = _retention(rq, rkt, proj, row(ret_norm_g))
    kv_mem = _mem_kv(mem, row(mem_norm_g), w_mem_kv, row(mem_k_norm_g))
    x1 = _merge(x, o_da, o_ret, proj, kv_mem, row(mem_q_norm_g), w_o_da.astype(BF16),
                w_o_ret.astype(BF16), w_o_mem.astype(BF16), w_out.astype(BF16))
    return _ffn(x1, row(ffn_norm_g), w_ffn_gate.astype(BF16), w_ffn_up.astype(BF16),
                w_ffn_down.astype(BF16))


def kernel(x, mem, positions, attn_norm_g, w_in, da_q_norm_g, da_k_norm_g, da_lambda_q1, da_lambda_k1,
           da_lambda_q2, da_lambda_k2, da_subln_g, ret_norm_g, mem_norm_g, w_mem_kv, mem_q_norm_g,
           mem_k_norm_g, w_o_da, w_o_ret, w_o_mem, w_out, ffn_norm_g, w_ffn_gate, w_ffn_up, w_ffn_down):
    batch, depth = x.shape[0], w_in.shape[0]
    outs = []
    for b in range(batch):
        xb = x[b]
        for l in range(depth):
            xb = _layer(xb, mem[b], positions[b], l, attn_norm_g[l], w_in[l], da_q_norm_g[l], da_k_norm_g[l],
                        da_lambda_q1[l], da_lambda_k1[l], da_lambda_q2[l], da_lambda_k2[l], da_subln_g[l],
                        ret_norm_g[l], mem_norm_g[l], w_mem_kv[l], mem_q_norm_g[l], mem_k_norm_g[l],
                        w_o_da[l], w_o_ret[l], w_o_mem[l], w_out[l], ffn_norm_g[l], w_ffn_gate[l],
                        w_ffn_up[l], w_ffn_down[l])
        outs.append(xb)
    return jnp.stack(outs)
```

```python
import functools
import math

import jax
import jax.numpy as jnp
from jax import lax
from jax.experimental import pallas as pl
from jax.experimental.pallas import tpu as pltpu

F32 = jnp.float32
BF16 = jnp.bfloat16

D_MODEL = 2048
EPS = 1e-6
LOG2E = math.log2(math.e)

DA_HEADS = 8
DA_QK_DIM = 64
DA_V_DIM = 128
DA_ROT_DIM = 16
ROPE_THETA = 500000.0

RET_HEADS = 8
RET_QK_DIM = 64
RET_V_DIM = 128
RET_ROT_BASE = 10000.0

MEM_HEADS = 4
MEM_HEAD_DIM = 256

D_FF = 5632

W_IN_GATE_OFF = 7168
GATE_OFF = 0
DA_Q_OFF = 6144
DA_K_OFF = 7168
DA_V_OFF = 8192
RET_Q_OFF = 9216
RET_K_OFF = 9728
RET_V_OFF = 10240
RET_G_OFF = 11264
MEM_Q_OFF = 12288
IN_COLS = 13312

LANES = 128
BF16_SUBLANES = 16
MXU_DIM = 256
VMEM_LIMIT = 60 * 1024 * 1024
NEG_BIG = -1e30
DA_VT_ROWS = DA_V_DIM + BF16_SUBLANES

IN_TM, IN_TN = 1024, 1024
PREP_TM = 512
DA_TQ, DA_TK = 512, 512
DA_HPS = 2
RET_CHUNK = 256
MERGE_TM = 256
FFN_TM, FFN_TF = 1024, 512


def _params(*sem):
    return pltpu.CompilerParams(dimension_semantics=sem, vmem_limit_bytes=VMEM_LIMIT)


def _rms(x, g):
    return x * lax.rsqrt(jnp.mean(x * x, axis=-1, keepdims=True) + EPS) * g


def _in_proj_body(x_ref, g_ref, w_ref, o_ref, h_ref):
    @pl.when(pl.program_id(1) == 0)
    def _():
        h_ref[...] = _rms(x_ref[...], g_ref[...]).astype(BF16)

    o_ref[...] = jnp.dot(h_ref[...], w_ref[...], preferred_element_type=F32).astype(o_ref.dtype)


def _in_proj(x, g, w):
    s = x.shape[0]
    return pl.pallas_call(
        _in_proj_body,
        out_shape=jax.ShapeDtypeStruct((s, IN_COLS), BF16),
        grid=(s // IN_TM, IN_COLS // IN_TN),
        in_specs=[
            pl.BlockSpec((IN_TM, D_MODEL), lambda i, j: (i, 0)),
            pl.BlockSpec((1, D_MODEL), lambda i, j: (0, 0)),
            pl.BlockSpec((D_MODEL, IN_TN), lambda i, j: (0, (j + W_IN_GATE_OFF // IN_TN) % (IN_COLS // IN_TN))),
        ],
        out_specs=pl.BlockSpec((IN_TM, IN_TN), lambda i, j: (i, j)),
        scratch_shapes=[pltpu.VMEM((IN_TM, D_MODEL), BF16)],
        compiler_params=_params("arbitrary", "arbitrary"),
        name="in_proj",
    )(x, g, w)


def _rotate(x, cos, sin_signed, first_half, half):
    w = x.shape[-1]
    partner = jnp.where(first_half, pltpu.roll(x, w - half, 1), pltpu.roll(x, half, 1))
    return x * cos + partner * sin_signed


def _qk_prep_body(pos_ref, invf_ref, dqk_ref, v_ref, rqk_ref, gain_ref, ones_ref,
                  qt_out, k_out, vt_out, rq_out, rkt_out):
    tm = pos_ref.shape[0]
    cw = MXU_DIM
    ang = pos_ref[...].astype(F32) * invf_ref[...]
    c, s = jnp.cos(ang), jnp.sin(ang)
    lane = lax.broadcasted_iota(jnp.int32, (tm, LANES), 1)
    low = lane < 64
    c_sw, s_sw = pltpu.roll(c, 64, 1), pltpu.roll(s, 64, 1)
    l64 = lane % 64
    ret_first = l64 < RET_QK_DIM // 2
    da_first = l64 < DA_ROT_DIM // 2
    c_r = jnp.where(low, c, c_sw)
    s_r = jnp.where(low, s, s_sw)
    s_r = jnp.where(ret_first, -s_r, s_r)
    c_d = jnp.where(low, c_sw, c)
    s_d = jnp.where(low, s_sw, s)
    s_d = jnp.where(da_first, -s_d, s_d)

    def wide(t):
        return jnp.concatenate([t, t], axis=1)

    c_r, s_r, c_d, s_d = wide(c_r), wide(s_r), wide(c_d), wide(s_d)
    ret_first, da_first = wide(ret_first), wide(da_first)

    ones_bd = ones_ref[...]
    n_q = (DA_K_OFF - DA_Q_OFF) // cw
    for ch in range(dqk_ref.shape[1] // cw):
        sl = slice(ch * cw, (ch + 1) * cw)
        x = dqk_ref[:, sl].astype(F32)
        ss = jnp.dot((x * x).astype(BF16), ones_bd, preferred_element_type=F32)
        xn = x * lax.rsqrt(ss * (1.0 / DA_QK_DIM) + EPS) * gain_ref[:, sl]
        y = _rotate(xn, c_d, s_d, da_first, DA_ROT_DIM // 2)
        if ch < n_q:
            qt_out[sl, :] = (y * (DA_QK_DIM ** -0.5 * LOG2E)).T.astype(BF16)
        else:
            k_out[:, (ch - n_q) * cw:(ch - n_q + 1) * cw] = y.astype(BF16)
    for h in range(DA_HEADS):
        vt_out[h, 0:DA_V_DIM, :] = v_ref[:, h * DA_V_DIM:(h + 1) * DA_V_DIM].astype(F32).T.astype(BF16)
        vt_out[h, DA_V_DIM:DA_VT_ROWS, :] = jnp.ones((DA_VT_ROWS - DA_V_DIM, tm), BF16)
    for ch in range(rqk_ref.shape[1] // cw):
        sl = slice(ch * cw, (ch + 1) * cw)
        x = rqk_ref[:, sl].astype(F32)
        y = _rotate(x, c_r, s_r, ret_first, RET_QK_DIM // 2)
        if ch * cw >= RET_K_OFF - RET_Q_OFF:
            y = y * (RET_QK_DIM ** -0.5)
        is_k = ch * cw >= RET_K_OFF - RET_Q_OFF
        for half in range(cw // LANES):
            yp = y[:, half * LANES:(half + 1) * LANES]
            slots = (jnp.where(low, yp, 0.0), jnp.where(low, pltpu.roll(yp, 64, 1), 0.0))
            for sub, piece in enumerate(slots):
                base = ((ch * cw // LANES + half) * 2 + sub) * LANES
                if is_k:
                    base -= RET_HEADS * LANES
                    rkt_out[base:base + LANES, :] = piece.T.astype(BF16)
                else:
                    rq_out[:, base:base + LANES] = piece.astype(BF16)


def _qk_prep(proj, positions, da_q_g, da_k_g):
    s = proj.shape[0]
    tm = PREP_TM
    half_r = RET_QK_DIM // 2
    half_d = DA_ROT_DIM // 2
    inv_r = RET_ROT_BASE ** (-jnp.arange(half_r, dtype=F32) * 2.0 / RET_QK_DIM)
    inv_d = ROPE_THETA ** (-jnp.arange(half_d, dtype=F32) * 2.0 / DA_ROT_DIM)
    inv = jnp.concatenate([inv_r, inv_r, inv_d, inv_d, jnp.zeros((64 - DA_ROT_DIM,), F32)]).reshape(1, LANES)
    gain = jnp.concatenate([jnp.tile(da_q_g.astype(F32), 2 * DA_HEADS),
                            jnp.tile(da_k_g.astype(F32), 2 * DA_HEADS)]).reshape(1, 2048)
    blk = jnp.arange(MXU_DIM) // DA_QK_DIM
    ones_bd = (blk[:, None] == blk[None, :]).astype(BF16)
    return pl.pallas_call(
        _qk_prep_body,
        out_shape=(jax.ShapeDtypeStruct((1024, s), BF16),
                   jax.ShapeDtypeStruct((s, 1024), BF16),
                   jax.ShapeDtypeStruct((DA_HEADS, DA_VT_ROWS, s), BF16),
                   jax.ShapeDtypeStruct((s, RET_HEADS * LANES), BF16),
                   jax.ShapeDtypeStruct((RET_HEADS * LANES, s), BF16)),
        grid=(s // tm,),
        in_specs=[
            pl.BlockSpec((tm, 1), lambda i: (i, 0)),
            pl.BlockSpec((1, LANES), lambda i: (0, 0)),
            pl.BlockSpec((tm, 2048), lambda i: (i, DA_Q_OFF // 2048)),
            pl.BlockSpec((tm, 1024), lambda i: (i, DA_V_OFF // 1024)),
            pl.BlockSpec((tm, 1024), lambda i: (i, RET_Q_OFF // 1024)),
            pl.BlockSpec((1, 2048), lambda i: (0, 0)),
            pl.BlockSpec((MXU_DIM, MXU_DIM), lambda i: (0, 0)),
        ],
        out_specs=(pl.BlockSpec((1024, tm), lambda i: (0, i)),
                   pl.BlockSpec((tm, 1024), lambda i: (i, 0)),
                   pl.BlockSpec((DA_HEADS, DA_VT_ROWS, tm), lambda i: (0, 0, i)),
                   pl.BlockSpec((tm, RET_HEADS * LANES), lambda i: (i, 0)),
                   pl.BlockSpec((RET_HEADS * LANES, tm), lambda i: (0, i))),
        compiler_params=_params("arbitrary"),
        name="qk_prep",
    )(positions.reshape(s, 1), inv, proj, proj, proj, gain, ones_bd)


def _diff_attn_body(lam_ref, g_ref, qt_ref, k_ref, vt_ref, o_ref, sa_ref, sb_ref, mxa_ref, mxb_ref,
                    m_ref, acc_ref, *, lambda_init):
    tq = qt_ref.shape[1]
    tk = sa_ref.shape[1]
    ratio = tq // tk
    qi = pl.program_id(1)
    last = k_ref.shape[0] // tk - 1
    row = lax.broadcasted_iota(jnp.int32, (LANES, tq), 0)
    qz = []
    for hh in range(DA_HPS):
        qf = qt_ref[hh * LANES:(hh + 1) * LANES, :].astype(F32)
        qz.append(jnp.where(row < DA_QK_DIM, qf, 0.0).astype(BF16))
        qz.append(jnp.where(row >= DA_QK_DIM, qf, 0.0).astype(BF16))
    m_ref[...] = jnp.full(m_ref.shape, NEG_BIG, F32)
    acc_ref[...] = jnp.zeros(acc_ref.shape, F32)
    rel = (lax.broadcasted_iota(jnp.int32, (tk, tq), 0) - lax.broadcasted_iota(jnp.int32, (tk, tq), 1))

    def scores(j, s_ref, mx_ref, masked, heads=range(DA_HPS)):
        off = pl.multiple_of(jnp.minimum(j, last) * tk, tk)
        for hh in heads:
            ks = k_ref[pl.ds(off, tk), hh * LANES:(hh + 1) * LANES]
            for st in (2 * hh, 2 * hh + 1):
                sc = jnp.dot(ks, qz[st], preferred_element_type=F32)
                if masked:
                    sc = jnp.where(rel <= qi * tq - j * tk, sc, NEG_BIG)
                s_ref[st] = sc
                mx_ref[st] = jnp.max(sc, axis=0, keepdims=True)

    def update(j, s_ref, mx_ref, heads=range(DA_HPS)):
        off = pl.multiple_of(jnp.minimum(j, last) * tk, tk)
        for hh in heads:
            vts = vt_ref[hh, :, pl.ds(off, tk)]
            for st in (2 * hh, 2 * hh + 1):
                m_old = m_ref[st]
                m_new = jnp.maximum(m_old, mx_ref[st])
                alpha = jnp.exp2(m_old - m_new)
                p = jnp.exp2(s_ref[st] - m_new).astype(BF16)
                acc_ref[st] = alpha * acc_ref[st] + jnp.dot(vts, p, preferred_element_type=F32)
                m_ref[st] = m_new

    scores(0, sa_ref, mxa_ref, True)
    n_pairs = jnp.maximum(qi * ratio - 1, 0) // 2

    def pair(t, carry):
        for hh in range(DA_HPS):
            scores(2 * t + 1, sb_ref, mxb_ref, False, (hh,))
            update(2 * t, sa_ref, mxa_ref, (hh,))
        for hh in range(DA_HPS):
            scores(2 * t + 2, sa_ref, mxa_ref, False, (hh,))
            update(2 * t + 1, sb_ref, mxb_ref, (hh,))
        return carry

    lax.fori_loop(0, n_pairs, pair, 0)
    e0 = 2 * n_pairs
    bufs = ((sa_ref, mxa_ref), (sb_ref, mxb_ref))
    for r in range(1, ratio + 1):
        for hh in range(DA_HPS):
            scores(e0 + r, *bufs[r % 2], True, (hh,))
            update(e0 + r - 1, *bufs[(r - 1) % 2], (hh,))
    has_extra = (qi + 1) * ratio - 1 - e0 > ratio

    @pl.when(has_extra)
    def _():
        for hh in range(DA_HPS):
            scores(e0 + ratio + 1, *bufs[(ratio + 1) % 2], True, (hh,))
            update(e0 + ratio, *bufs[ratio % 2], (hh,))
        update(e0 + ratio + 1, *bufs[(ratio + 1) % 2])

    @pl.when(jnp.logical_not(has_extra))
    def _():
        update(e0 + ratio, *bufs[ratio % 2])

    lq1, lk1 = lam_ref[0:1, :], lam_ref[1:2, :]
    lq2, lk2 = lam_ref[2:3, :], lam_ref[3:4, :]
    lam = (jnp.exp(jnp.sum(lq1 * lk1, axis=-1, keepdims=True))
           - jnp.exp(jnp.sum(lq2 * lk2, axis=-1, keepdims=True)) + lambda_init)
    dv = DA_V_DIM
    for hh in range(DA_HPS):
        a1, a2 = acc_ref[2 * hh], acc_ref[2 * hh + 1]
        ot = a1[0:dv] / a1[dv:dv + 1] - lam * (a2[0:dv] / a2[dv:dv + 1])
        ot = ot * lax.rsqrt(jnp.mean(ot * ot, axis=0, keepdims=True) + EPS)
        o_ref[:, hh * dv:(hh + 1) * dv] = (ot.T * g_ref[...] * (1.0 - lambda_init)).astype(o_ref.dtype)


def _diff_attn(qt, k, vt, lam_params, subln_g, lambda_init):
    s = k.shape[0]
    tq = DA_TQ
    hps = DA_HPS
    n_st = 2 * hps
    body = functools.partial(_diff_attn_body, lambda_init=lambda_init)
    return pl.pallas_call(
        body,
        out_shape=jax.ShapeDtypeStruct((s, DA_HEADS * DA_V_DIM), BF16),
        grid=(DA_HEADS // hps, s // tq),
        in_specs=[
            pl.BlockSpec((4, DA_QK_DIM), lambda h, i: (0, 0)),
            pl.BlockSpec((1, DA_V_DIM), lambda h, i: (0, 0)),
            pl.BlockSpec((hps * LANES, tq), lambda h, i: (h, i)),
            pl.BlockSpec((s, hps * LANES), lambda h, i: (0, h)),
            pl.BlockSpec((hps, DA_VT_ROWS, s), lambda h, i: (h, 0, 0)),
        ],
        out_specs=pl.BlockSpec((tq, hps * DA_V_DIM), lambda h, i: (i, h)),
        scratch_shapes=[pltpu.VMEM((n_st, DA_TK, tq), F32), pltpu.VMEM((n_st, DA_TK, tq), F32),
                        pltpu.VMEM((n_st, 1, tq), F32), pltpu.VMEM((n_st, 1, tq), F32),
                        pltpu.VMEM((n_st, 1, tq), F32), pltpu.VMEM((n_st, DA_VT_ROWS, tq), F32)],
        compiler_params=_params("arbitrary", "arbitrary"),
        name="diff_attn",
    )(lam_params, subln_g, qt, k, vt)


def _retention_body(q_ref, k_ref, v_ref, gate_ref, g_ref, o_ref, state_ref, decay_ref, qdec_ref, kdec_ref):
    c = q_ref.shape[0]
    log_gamma = [math.log(1.0 - 2.0 ** (-5.0 - h)) for h in range(RET_HEADS)]

    @pl.when(pl.program_id(0) == 0)
    def _():
        state_ref[...] = jnp.zeros(state_ref.shape, F32)
        row = lax.broadcasted_iota(jnp.int32, (c, c), 0)
        col = lax.broadcasted_iota(jnp.int32, (c, c), 1)
        rel = (row - col).astype(F32)
        idx = lax.broadcasted_iota(jnp.int32, (c, 1), 0).astype(F32)
        idx_row = lax.broadcasted_iota(jnp.int32, (1, c), 1).astype(F32)
        for h in range(RET_HEADS):
            decay_ref[h] = jnp.where(rel >= 0, jnp.exp(log_gamma[h] * jnp.maximum(rel, 0.0)), 0.0)
            qdec_ref[h] = jnp.exp(log_gamma[h] * (idx + 1.0))
            kdec_ref[h] = jnp.exp(log_gamma[h] * (c - 1.0 - idx_row))

    for h in range(RET_HEADS):
        qb = q_ref[:, h * LANES:(h + 1) * LANES]
        kt = k_ref[h * LANES:(h + 1) * LANES, :]
        vh = v_ref[:, h * RET_V_DIM:(h + 1) * RET_V_DIM]
        scores = jnp.dot(qb, kt, preferred_element_type=F32) * decay_ref[h]
        o_in = jnp.dot(scores.astype(BF16), vh, preferred_element_type=F32)
        state = state_ref[h]
        o_cross = jnp.dot(qb, state.astype(BF16), preferred_element_type=F32) * qdec_ref[h]
        kd = (kt.astype(F32) * kdec_ref[h]).astype(BF16)
        state_ref[h] = state * math.exp(log_gamma[h] * c) + jnp.dot(kd, vh, preferred_element_type=F32)
        o = _rms(o_in + o_cross, g_ref[...])
        gate = gate_ref[:, h * RET_V_DIM:(h + 1) * RET_V_DIM].astype(F32)
        o_ref[:, h * RET_V_DIM:(h + 1) * RET_V_DIM] = (gate * jax.nn.sigmoid(gate) * o).astype(o_ref.dtype)


def _retention(rq, rkt, proj, norm_g):
    s = rq.shape[0]
    c = RET_CHUNK
    w = RET_HEADS * RET_V_DIM
    return pl.pallas_call(
        _retention_body,
        out_shape=jax.ShapeDtypeStruct((s, w), BF16),
        grid=(s // c,),
        in_specs=[
            pl.BlockSpec((c, RET_HEADS * LANES), lambda i: (i, 0)),
            pl.BlockSpec((RET_HEADS * LANES, c), lambda i: (0, i)),
            pl.BlockSpec((c, w), lambda i: (i, RET_V_OFF // w)),
            pl.BlockSpec((c, w), lambda i: (i, RET_G_OFF // w)),
            pl.BlockSpec((1, RET_V_DIM), lambda i: (0, 0)),
        ],
        out_specs=pl.BlockSpec((c, w), lambda i: (i, 0)),
        scratch_shapes=[pltpu.VMEM((RET_HEADS, LANES, RET_V_DIM), F32),
                        pltpu.VMEM((RET_HEADS, c, c), F32),
                        pltpu.VMEM((RET_HEADS, c, 1), F32),
                        pltpu.VMEM((RET_HEADS, 1, c), F32)],
        compiler_params=_params("arbitrary"),
        name="retention",
    )(rq, rkt, proj, proj, norm_g)


def _mem_kv_body(mem_ref, g_ref, w_ref, kg_ref, o_ref):
    m = _rms(mem_ref[...], g_ref[...]).astype(BF16)
    kv = jnp.dot(m, w_ref[...].astype(BF16), preferred_element_type=F32)
    kn = _rms(kv, kg_ref[...]) * (MEM_HEAD_DIM ** -0.5)
    o_ref[...] = jnp.where(pl.program_id(0) < MEM_HEADS, kn, kv).astype(o_ref.dtype)


def _mem_kv(mem, mem_norm_g, w_mem_kv, k_g):
    n = mem.shape[0]
    hd = MEM_HEAD_DIM
    return pl.pallas_call(
        _mem_kv_body,
        out_shape=jax.ShapeDtypeStruct((n, 2 * MEM_HEADS * hd), BF16),
        grid=(2 * MEM_HEADS,),
        in_specs=[
            pl.BlockSpec((n, D_MODEL), lambda j: (0, 0)),
            pl.BlockSpec((1, D_MODEL), lambda j: (0, 0)),
            pl.BlockSpec((D_MODEL, hd), lambda j: (0, j)),
            pl.BlockSpec((1, hd), lambda j: (0, 0)),
        ],
        out_specs=pl.BlockSpec((n, hd), lambda j: (0, j)),
        compiler_params=_params("arbitrary"),
        name="mem_kv",
    )(mem, mem_norm_g, w_mem_kv, k_g)


def _merge_body(x_ref, oda_ref, oret_ref, mq_ref, gates_ref, kv_ref, qg_ref,
                wda_ref, wret_ref, wmem_ref, wout_ref, o_ref, omem_ref):
    hd = MEM_HEAD_DIM
    for h in range(MEM_HEADS):
        q = _rms(mq_ref[:, h * hd:(h + 1) * hd].astype(F32), qg_ref[...]).astype(BF16)
        k = kv_ref[:, h * hd:(h + 1) * hd]
        v = kv_ref[:, (MEM_HEADS + h) * hd:(MEM_HEADS + h + 1) * hd]
        sc = lax.dot_general(q, k, (((1,), (1,)), ((), ())), preferred_element_type=F32)
        p = jnp.exp(sc - jnp.max(sc, axis=-1, keepdims=True))
        o = jnp.dot(p.astype(BF16), v, preferred_element_type=F32) / jnp.sum(p, axis=-1, keepdims=True)
        omem_ref[:, h * hd:(h + 1) * hd] = o.astype(BF16)

    def branch(b, act, w_ref):
        gate = gates_ref[:, b * D_MODEL:(b + 1) * D_MODEL].astype(F32)
        return jax.nn.sigmoid(gate) * jnp.dot(act, w_ref[...], preferred_element_type=F32)

    merged = (branch(0, oda_ref[...], wda_ref) + branch(1, oret_ref[...], wret_ref)
              + branch(2, omem_ref[...], wmem_ref))
    o_ref[...] = x_ref[...] + jnp.dot(merged.astype(BF16), wout_ref[...], preferred_element_type=F32)


def _merge(x, o_da, o_ret, proj, kv_mem, mem_q_g, w_o_da, w_o_ret, w_o_mem, w_out):
    s = x.shape[0]
    tm = MERGE_TM
    n_mem = kv_mem.shape[0]
    bw = 1024
    gw = 3 * D_MODEL
    row = lambda i: (i, 0)
    resident = lambda shape: pl.BlockSpec(shape, lambda i: (0, 0), pipeline_mode=pl.Buffered(1))
    return pl.pallas_call(
        _merge_body,
        out_shape=jax.ShapeDtypeStruct((s, D_MODEL), F32),
        grid=(s // tm,),
        in_specs=[
            pl.BlockSpec((tm, D_MODEL), row),
            pl.BlockSpec((tm, bw), row),
            pl.BlockSpec((tm, bw), row),
            pl.BlockSpec((tm, bw), lambda i: (i, MEM_Q_OFF // bw)),
            pl.BlockSpec((tm, gw), lambda i: (i, GATE_OFF // gw)),
            resident((n_mem, 2 * bw)),
            resident((1, MEM_HEAD_DIM)),
            resident((bw, D_MODEL)),
            resident((bw, D_MODEL)),
            resident((bw, D_MODEL)),
            resident((D_MODEL, D_MODEL)),
        ],
        out_specs=pl.BlockSpec((tm, D_MODEL), row),
        scratch_shapes=[pltpu.VMEM((tm, bw), BF16)],
        compiler_params=_params("arbitrary"),
        name="merge",
    )(x, o_da, o_ret, proj, proj, kv_mem, mem_q_g, w_o_da, w_o_ret, w_o_mem, w_out)


def _ffn_body(x_ref, g_ref, wg_ref, wu_ref, wd_ref, o_ref, h_ref):
    @pl.when(pl.program_id(1) == 0)
    def _():
        x = x_ref[...]
        o_ref[...] = x
        h_ref[...] = _rms(x, g_ref[...]).astype(BF16)

    h = h_ref[...]
    gate = jnp.dot(h, wg_ref[...], preferred_element_type=F32)
    up = jnp.dot(h, wu_ref[...], preferred_element_type=F32)
    act = (gate * jax.nn.sigmoid(gate) * up).astype(BF16)
    o_ref[...] += jnp.dot(act, wd_ref[...], preferred_element_type=F32)


def _ffn(x, g, w_gate, w_up, w_down):
    s = x.shape[0]
    tm, tf = FFN_TM, FFN_TF
    return pl.pallas_call(
        _ffn_body,
        out_shape=jax.ShapeDtypeStruct((s, D_MODEL), F32),
        grid=(s // tm, D_FF // tf),
        in_specs=[
            pl.BlockSpec((tm, D_MODEL), lambda i, j: (i, 0)),
            pl.BlockSpec((1, D_MODEL), lambda i, j: (0, 0)),
            pl.BlockSpec((D_MODEL, tf), lambda i, j: (0, j)),
            pl.BlockSpec((D_MODEL, tf), lambda i, j: (0, j)),
            pl.BlockSpec((tf, D_MODEL), lambda i, j: (j, 0)),
        ],
        out_specs=pl.BlockSpec((tm, D_MODEL), lambda i, j: (i, 0)),
        scratch_shapes=[pltpu.VMEM((tm, D_MODEL), BF16)],
        compiler_params=_params("arbitrary", "arbitrary"),
        name="ffn",
    )(x, g, w_gate, w_up, w_down)


def _layer(x, mem, positions, l, attn_norm_g, w_in, da_q_norm_g, da_k_norm_g, da_lambda_q1, da_lambda_k1,
           da_lambda_q2, da_lambda_k2, da_subln_g, ret_norm_g, mem_norm_g, w_mem_kv, mem_q_norm_g,
           mem_k_norm_g, w_o_da, w_o_ret, w_o_mem, w_out, ffn_norm_g, w_ffn_gate, w_ffn_up, w_ffn_down):
    lambda_init = 0.8 - 0.6 * math.exp(-0.3 * l)
    row = lambda a: a.astype(F32).reshape(1, -1)
    proj = _in_proj(x, row(attn_norm_g), w_in.astype(BF16))
    da_qt, da_k, da_vt, rq, rkt = _qk_prep(proj, positions, da_q_norm_g, da_k_norm_g)
    lam_params = jnp.stack([da_lambda_q1, da_lambda_k1, da_lambda_q2, da_lambda_k2]).astype(F32)
    o_da = _diff_attn(da_qt, da_k, da_vt, lam_params, row(da_subln_g), lambda_init)
    o_ret = _retention(rq, rkt, proj, row(ret_norm_g))
    kv_mem = _mem_kv(mem, row(mem_norm_g), w_mem_kv, row(mem_k_norm_g))
    x1 = _merge(x, o_da, o_ret, proj, kv_mem, row(mem_q_norm_g), w_o_da.astype(BF16),
                w_o_ret.astype(BF16), w_o_mem.astype(BF16), w_out.astype(BF16))
    return _ffn(x1, row(ffn_norm_g), w_ffn_gate.astype(BF16), w_ffn_up.astype(BF16),
                w_ffn_down.astype(BF16))


def kernel(x, mem, positions, attn_norm_g, w_in, da_q_norm_g, da_k_norm_g, da_lambda_q1, da_lambda_k1,
           da_lambda_q2, da_lambda_k2, da_subln_g, ret_norm_g, mem_norm_g, w_mem_kv, mem_q_norm_g,
           mem_k_norm_g, w_o_da, w_o_ret, w_o_mem, w_out, ffn_norm_g, w_ffn_gate, w_ffn_up, w_ffn_down):
    batch, depth = x.shape[0], w_in.shape[0]
    outs = []
    for b in range(batch):
        xb = x[b]
        for l in range(depth):
            xb = _layer(xb, mem[b], positions[b], l, attn_norm_g[l], w_in[l], da_q_norm_g[l], da_k_norm_g[l],
                        da_lambda_q1[l], da_lambda_k1[l], da_lambda_q2[l], da_lambda_k2[l], da_subln_g[l],
                        ret_norm_g[l], mem_norm_g[l], w_mem_kv[l], mem_q_norm_g[l], mem_k_norm_g[l],
                        w_o_da[l], w_o_ret[l], w_o_mem[l], w_out[l], ffn_norm_g[l], w_ffn_gate[l],
                        w_ffn_up[l], w_ffn_down[l])
        outs.append(xb)
    return jnp.stack(outs)
```

```python
import functools
import math

import jax
import jax.numpy as jnp
from jax import lax
from jax.experimental import pallas as pl
from jax.experimental.pallas import tpu as pltpu

F32 = jnp.float32
BF16 = jnp.bfloat16

D_MODEL = 2048
EPS = 1e-6
LOG2E = math.log2(math.e)

DA_HEADS = 8
DA_QK_DIM = 64
DA_V_DIM = 128
DA_ROT_DIM = 16
ROPE_THETA = 500000.0

RET_HEADS = 8
RET_QK_DIM = 64
RET_V_DIM = 128
RET_ROT_BASE = 10000.0

MEM_HEADS = 4
MEM_HEAD_DIM = 256

D_FF = 5632

W_IN_GATE_OFF = 7168
GATE_OFF = 0
DA_Q_OFF = 6144
DA_K_OFF = 7168
DA_V_OFF = 8192
RET_Q_OFF = 9216
RET_K_OFF = 9728
RET_V_OFF = 10240
RET_G_OFF = 11264
MEM_Q_OFF = 12288
IN_COLS = 13312

LANES = 128
BF16_SUBLANES = 16
MXU_DIM = 256
VMEM_LIMIT = 60 * 1024 * 1024
NEG_BIG = -1e30
DA_VT_ROWS = DA_V_DIM + BF16_SUBLANES

IN_TM, IN_TN = 1024, 1024
PREP_TM = 512
DA_TQ, DA_TK = 512, 512
DA_HPS = 2
RET_CHUNK = 256
MERGE_TM = 256
FFN_TM, FFN_TF = 1024, 512


def _params(*sem):
    return pltpu.CompilerParams(dimension_semantics=sem, vmem_limit_bytes=VMEM_LIMIT)


def _rms(x, g):
    return x * lax.rsqrt(jnp.mean(x * x, axis=-1, keepdims=True) + EPS) * g


def _in_proj_body(x_ref, g_ref, w_ref, o_ref, h_ref):
    @pl.when(pl.program_id(1) == 0)
    def _():
        h_ref[...] = _rms(x_ref[...], g_ref[...]).astype(BF16)

    o_ref[...] = jnp.dot(h_ref[...], w_ref[...], preferred_element_type=F32).astype(o_ref.dtype)


def _in_proj(x, g, w):
    s = x.shape[0]
    return pl.pallas_call(
        _in_proj_body,
        out_shape=jax.ShapeDtypeStruct((s, IN_COLS), BF16),
        grid=(s // IN_TM, IN_COLS // IN_TN),
        in_specs=[
            pl.BlockSpec((IN_TM, D_MODEL), lambda i, j: (i, 0)),
            pl.BlockSpec((1, D_MODEL), lambda i, j: (0, 0)),
            pl.BlockSpec((D_MODEL, IN_TN), lambda i, j: (0, (j + W_IN_GATE_OFF // IN_TN) % (IN_COLS // IN_TN))),
        ],
        out_specs=pl.BlockSpec((IN_TM, IN_TN), lambda i, j: (i, j)),
        scratch_shapes=[pltpu.VMEM((IN_TM, D_MODEL), BF16)],
        compiler_params=_params("arbitrary", "arbitrary"),
        name="in_proj",
    )(x, g, w)


def _rotate(x, cos, sin_signed, first_half, half):
    w = x.shape[-1]
    partner = jnp.where(first_half, pltpu.roll(x, w - half, 1), pltpu.roll(x, half, 1))
    return x * cos + partner * sin_signed


def _qk_prep_body(pos_ref, invf_ref, dqk_ref, v_ref, rqk_ref, gain_ref, ones_ref,
                  qt_out, k_out, vt_out, rq_out, rkt_out):
    tm = pos_ref.shape[0]
    cw = MXU_DIM
    ang = pos_ref[...].astype(F32) * invf_ref[...]
    c, s = jnp.cos(ang), jnp.sin(ang)
    lane = lax.broadcasted_iota(jnp.int32, (tm, LANES), 1)
    low = lane < 64
    c_sw, s_sw = pltpu.roll(c, 64, 1), pltpu.roll(s, 64, 1)
    l64 = lane % 64
    ret_first = l64 < RET_QK_DIM // 2
    da_first = l64 < DA_ROT_DIM // 2
    c_r = jnp.where(low, c, c_sw)
    s_r = jnp.where(low, s, s_sw)
    s_r = jnp.where(ret_first, -s_r, s_r)
    c_d = jnp.where(low, c_sw, c)
    s_d = jnp.where(low, s_sw, s)
    s_d = jnp.where(da_first, -s_d, s_d)

    def wide(t):
        return jnp.concatenate([t, t], axis=1)

    c_r, s_r, c_d, s_d = wide(c_r), wide(s_r), wide(c_d), wide(s_d)
    ret_first, da_first = wide(ret_first), wide(da_first)

    ones_bd = ones_ref[...]
    n_q = (DA_K_OFF - DA_Q_OFF) // cw
    for ch in range(dqk_ref.shape[1] // cw):
        sl = slice(ch * cw, (ch + 1) * cw)
        x = dqk_ref[:, sl].astype(F32)
        ss = jnp.dot((x * x).astype(BF16), ones_bd, preferred_element_type=F32)
        xn = x * lax.rsqrt(ss * (1.0 / DA_QK_DIM) + EPS) * gain_ref[:, sl]
        y = _rotate(xn, c_d, s_d, da_first, DA_ROT_DIM // 2)
        if ch < n_q:
            qt_out[sl, :] = (y * (DA_QK_DIM ** -0.5 * LOG2E)).T.astype(BF16)
        else:
            k_out[:, (ch - n_q) * cw:(ch - n_q + 1) * cw] = y.astype(BF16)
    for h in range(DA_HEADS):
        vt_out[h, 0:DA_V_DIM, :] = v_ref[:, h * DA_V_DIM:(h + 1) * DA_V_DIM].astype(F32).T.astype(BF16)
        vt_out[h, DA_V_DIM:DA_VT_ROWS, :] = jnp.ones((DA_VT_ROWS - DA_V_DIM, tm), BF16)
    for ch in range(rqk_ref.shape[1] // cw):
        sl = slice(ch * cw, (ch + 1) * cw)
        x = rqk_ref[:, sl].astype(F32)
        y = _rotate(x, c_r, s_r, ret_first, RET_QK_DIM // 2)
        if ch * cw >= RET_K_OFF - RET_Q_OFF:
            y = y * (RET_QK_DIM ** -0.5)
        is_k = ch * cw >= RET_K_OFF - RET_Q_OFF
        for half in range(cw // LANES):
            yp = y[:, half * LANES:(half + 1) * LANES]
            slots = (jnp.where(low, yp, 0.0), jnp.where(low, pltpu.roll(yp, 64, 1), 0.0))
            for sub, piece in enumerate(slots):
                base = ((ch * cw // LANES + half) * 2 + sub) * LANES
                if is_k:
                    base -= RET_HEADS * LANES
                    rkt_out[base:base + LANES, :] = piece.T.astype(BF16)
                else:
                    rq_out[:, base:base + LANES] = piece.astype(BF16)


def _qk_prep(proj, positions, da_q_g, da_k_g):
    s = proj.shape[0]
    tm = PREP_TM
    half_r = RET_QK_DIM // 2
    half_d = DA_ROT_DIM // 2
    inv_r = RET_ROT_BASE ** (-jnp.arange(half_r, dtype=F32) * 2.0 / RET_QK_DIM)
    inv_d = ROPE_THETA ** (-jnp.arange(half_d, dtype=F32) * 2.0 / DA_ROT_DIM)
    inv = jnp.concatenate([inv_r, inv_r, inv_d, inv_d, jnp.zeros((64 - DA_ROT_DIM,), F32)]).reshape(1, LANES)
    gain = jnp.concatenate([jnp.tile(da_q_g.astype(F32), 2 * DA_HEADS),
                            jnp.tile(da_k_g.astype(F32), 2 * DA_HEADS)]).reshape(1, 2048)
    blk = jnp.arange(MXU_DIM) // DA_QK_DIM
    ones_bd = (blk[:, None] == blk[None, :]).astype(BF16)
    return pl.pallas_call(
        _qk_prep_body,
        out_shape=(jax.ShapeDtypeStruct((1024, s), BF16),
                   jax.ShapeDtypeStruct((s, 1024), BF16),
                   jax.ShapeDtypeStruct((DA_HEADS, DA_VT_ROWS, s), BF16),
                   jax.ShapeDtypeStruct((s, RET_HEADS * LANES), BF16),
                   jax.ShapeDtypeStruct((RET_HEADS * LANES, s), BF16)),
        grid=(s // tm,),
        in_specs=[
            pl.BlockSpec((tm, 1), lambda i: (i, 0)),
            pl.BlockSpec((1, LANES), lambda i: (0, 0)),
            pl.BlockSpec((tm, 2048), lambda i: (i, DA_Q_OFF // 2048)),
            pl.BlockSpec((tm, 1024), lambda i: (i, DA_V_OFF // 1024)),
            pl.BlockSpec((tm, 1024), lambda i: (i, RET_Q_OFF // 1024)),
            pl.BlockSpec((1, 2048), lambda i: (0, 0)),
            pl.BlockSpec((MXU_DIM, MXU_DIM), lambda i: (0, 0)),
        ],
        out_specs=(pl.BlockSpec((1024, tm), lambda i: (0, i)),
                   pl.BlockSpec((tm, 1024), lambda i: (i, 0)),
                   pl.BlockSpec((DA_HEADS, DA_VT_ROWS, tm), lambda i: (0, 0, i)),
                   pl.BlockSpec((tm, RET_HEADS * LANES), lambda i: (i, 0)),
                   pl.BlockSpec((RET_HEADS * LANES, tm), lambda i: (0, i))),
        compiler_params=_params("arbitrary"),
        name="qk_prep",
    )(positions.reshape(s, 1), inv, proj, proj, proj, gain, ones_bd)


def _diff_attn_body(lam_ref, g_ref, qt_ref, k_ref, vt_ref, o_ref, sa_ref, sb_ref, mxa_ref, mxb_ref,
                    m_ref, acc_ref, *, lambda_init):
    tq = qt_ref.shape[1]
    tk = sa_ref.shape[1]
    ratio = tq // tk
    qi = pl.program_id(1)
    last = k_ref.shape[0] // tk - 1
    row = lax.broadcasted_iota(jnp.int32, (LANES, tq), 0)
    qz = []
    for hh in range(DA_HPS):
        qf = qt_ref[hh * LANES:(hh + 1) * LANES, :].astype(F32)
        qz.append(jnp.where(row < DA_QK_DIM, qf, 0.0).astype(BF16))
        qz.append(jnp.where(row >= DA_QK_DIM, qf, 0.0).astype(BF16))
    m_ref[...] = jnp.full(m_ref.shape, NEG_BIG, F32)
    acc_ref[...] = jnp.zeros(acc_ref.shape, F32)
    rel = (lax.broadcasted_iota(jnp.int32, (tk, tq), 0) - lax.broadcasted_iota(jnp.int32, (tk, tq), 1))

    def scores(j, s_ref, mx_ref, masked, heads=range(DA_HPS), maps=(0, 1)):
        off = pl.multiple_of(jnp.minimum(j, last) * tk, tk)
        for hh in heads:
            ks = k_ref[pl.ds(off, tk), hh * LANES:(hh + 1) * LANES]
            for st in [2 * hh + mp for mp in maps]:
                sc = jnp.dot(ks, qz[st], preferred_element_type=F32)
                if masked:
                    sc = jnp.where(rel <= qi * tq - j * tk, sc, NEG_BIG)
                s_ref[st] = sc
                mx_ref[st] = jnp.max(sc, axis=0, keepdims=True)

    def update(j, s_ref, mx_ref, heads=range(DA_HPS), maps=(0, 1)):
        off = pl.multiple_of(jnp.minimum(j, last) * tk, tk)
        for hh in heads:
            vts = vt_ref[hh, :, pl.ds(off, tk)]
            for st in [2 * hh + mp for mp in maps]:
                m_old = m_ref[st]
                m_new = jnp.maximum(m_old, mx_ref[st])
                alpha = jnp.exp2(m_old - m_new)
                p = jnp.exp2(s_ref[st] - m_new).astype(BF16)
                acc_ref[st] = alpha * acc_ref[st] + jnp.dot(vts, p, preferred_element_type=F32)
                m_ref[st] = m_new

    scores(0, sa_ref, mxa_ref, True)
    n_pairs = jnp.maximum(qi * ratio - 1, 0) // 2

    def pair(t, carry):
        for hh in range(DA_HPS):
            for mp in range(2):
                scores(2 * t + 1, sb_ref, mxb_ref, False, (hh,), (mp,))
                update(2 * t, sa_ref, mxa_ref, (hh,), (mp,))
        for hh in range(DA_HPS):
            for mp in range(2):
                scores(2 * t + 2, sa_ref, mxa_ref, False, (hh,), (mp,))
                update(2 * t + 1, sb_ref, mxb_ref, (hh,), (mp,))
        return carry

    lax.fori_loop(0, n_pairs, pair, 0)
    e0 = 2 * n_pairs
    bufs = ((sa_ref, mxa_ref), (sb_ref, mxb_ref))
    for r in range(1, ratio + 1):
        for hh in range(DA_HPS):
            for mp in range(2):
                scores(e0 + r, *bufs[r % 2], True, (hh,), (mp,))
                update(e0 + r - 1, *bufs[(r - 1) % 2], (hh,), (mp,))
    has_extra = (qi + 1) * ratio - 1 - e0 > ratio

    @pl.when(has_extra)
    def _():
        for hh in range(DA_HPS):
            for mp in range(2):
                scores(e0 + ratio + 1, *bufs[(ratio + 1) % 2], True, (hh,), (mp,))
                update(e0 + ratio, *bufs[ratio % 2], (hh,), (mp,))
        update(e0 + ratio + 1, *bufs[(ratio + 1) % 2])

    @pl.when(jnp.logical_not(has_extra))
    def _():
        update(e0 + ratio, *bufs[ratio % 2])

    lq1, lk1 = lam_ref[0:1, :], lam_ref[1:2, :]
    lq2, lk2 = lam_ref[2:3, :], lam_ref[3:4, :]
    lam = (jnp.exp(jnp.sum(lq1 * lk1, axis=-1, keepdims=True))
           - jnp.exp(jnp.sum(lq2 * lk2, axis=-1, keepdims=True)) + lambda_init)
    dv = DA_V_DIM
    for hh in range(DA_HPS):
        a1, a2 = acc_ref[2 * hh], acc_ref[2 * hh + 1]
        ot = a1[0:dv] / a1[dv:dv + 1] - lam * (a2[0:dv] / a2[dv:dv + 1])
        ot = ot * lax.rsqrt(jnp.mean(ot * ot, axis=0, keepdims=True) + EPS)
        o_ref[:, hh * dv:(hh + 1) * dv] = (ot.T * g_ref[...] * (1.0 - lambda_init)).astype(o_ref.dtype)


def _diff_attn(qt, k, vt, lam_params, subln_g, lambda_init):
    s = k.shape[0]
    tq = DA_TQ
    hps = DA_HPS
    n_st = 2 * hps
    body = functools.partial(_diff_attn_body, lambda_init=lambda_init)
    return pl.pallas_call(
        body,
        out_shape=jax.ShapeDtypeStruct((s, DA_HEADS * DA_V_DIM), BF16),
        grid=(DA_HEADS // hps, s // tq),
        in_specs=[
            pl.BlockSpec((4, DA_QK_DIM), lambda h, i: (0, 0)),
            pl.BlockSpec((1, DA_V_DIM), lambda h, i: (0, 0)),
            pl.BlockSpec((hps * LANES, tq), lambda h, i: (h, i)),
            pl.BlockSpec((s, hps * LANES), lambda h, i: (0, h)),
            pl.BlockSpec((hps, DA_VT_ROWS, s), lambda h, i: (h, 0, 0)),
        ],
        out_specs=pl.BlockSpec((tq, hps * DA_V_DIM), lambda h, i: (i, h)),
        scratch_shapes=[pltpu.VMEM((n_st, DA_TK, tq), F32), pltpu.VMEM((n_st, DA_TK, tq), F32),
                        pltpu.VMEM((n_st, 1, tq), F32), pltpu.VMEM((n_st, 1, tq), F32),
                        pltpu.VMEM((n_st, 1, tq), F32), pltpu.VMEM((n_st, DA_VT_ROWS, tq), F32)],
        compiler_params=_params("arbitrary", "arbitrary"),
        name="diff_attn",
    )(lam_params, subln_g, qt, k, vt)


def _retention_body(q_ref, k_ref, v_ref, gate_ref, g_ref, o_ref, state_ref, decay_ref, qdec_ref, kdec_ref):
    c = q_ref.shape[0]
    log_gamma = [math.log(1.0 - 2.0 ** (-5.0 - h)) for h in range(RET_HEADS)]

    @pl.when(pl.program_id(0) == 0)
    def _():
        state_ref[...] = jnp.zeros(state_ref.shape, F32)
        row = lax.broadcasted_iota(jnp.int32, (c, c), 0)
        col = lax.broadcasted_iota(jnp.int32, (c, c), 1)
        rel = (row - col).astype(F32)
        idx = lax.broadcasted_iota(jnp.int32, (c, 1), 0).astype(F32)
        idx_row = lax.broadcasted_iota(jnp.int32, (1, c), 1).astype(F32)
        for h in range(RET_HEADS):
            decay_ref[h] = jnp.where(rel >= 0, jnp.exp(log_gamma[h] * jnp.maximum(rel, 0.0)), 0.0)
            qdec_ref[h] = jnp.exp(log_gamma[h] * (idx + 1.0))
            kdec_ref[h] = jnp.exp(log_gamma[h] * (c - 1.0 - idx_row))

    for h in range(RET_HEADS):
        qb = q_ref[:, h * LANES:(h + 1) * LANES]
        kt = k_ref[h * LANES:(h + 1) * LANES, :]
        vh = v_ref[:, h * RET_V_DIM:(h + 1) * RET_V_DIM]
        scores = jnp.dot(qb, kt, preferred_element_type=F32) * decay_ref[h]
        o_in = jnp.dot(scores.astype(BF16), vh, preferred_element_type=F32)
        state = state_ref[h]
        o_cross = jnp.dot(qb, state.astype(BF16), preferred_element_type=F32) * qdec_ref[h]
        kd = (kt.astype(F32) * kdec_ref[h]).astype(BF16)
        state_ref[h] = state * math.exp(log_gamma[h] * c) + jnp.dot(kd, vh, preferred_element_type=F32)
        o = _rms(o_in + o_cross, g_ref[...])
        gate = gate_ref[:, h * RET_V_DIM:(h + 1) * RET_V_DIM].astype(F32)
        o_ref[:, h * RET_V_DIM:(h + 1) * RET_V_DIM] = (gate * jax.nn.sigmoid(gate) * o).astype(o_ref.dtype)


def _retention(rq, rkt, proj, norm_g):
    s = rq.shape[0]
    c = RET_CHUNK
    w = RET_HEADS * RET_V_DIM
    return pl.pallas_call(
        _retention_body,
        out_shape=jax.ShapeDtypeStruct((s, w), BF16),
        grid=(s // c,),
        in_specs=[
            pl.BlockSpec((c, RET_HEADS * LANES), lambda i: (i, 0)),
            pl.BlockSpec((RET_HEADS * LANES, c), lambda i: (0, i)),
            pl.BlockSpec((c, w), lambda i: (i, RET_V_OFF // w)),
            pl.BlockSpec((c, w), lambda i: (i, RET_G_OFF // w)),
            pl.BlockSpec((1, RET_V_DIM), lambda i: (0, 0)),
        ],
        out_specs=pl.BlockSpec((c, w), lambda i: (i, 0)),
        scratch_shapes=[pltpu.VMEM((RET_HEADS, LANES, RET_V_DIM), F32),
                        pltpu.VMEM((RET_HEADS, c, c), F32),
                        pltpu.VMEM((RET_HEADS, c, 1), F32),
                        pltpu.VMEM((RET_HEADS, 1, c), F32)],
        compiler_params=_params("arbitrary"),
        name="retention",
    )(rq, rkt, proj, proj, norm_g)


def _mem_kv_body(mem_ref, g_ref, w_ref, kg_ref, o_ref):
    m = _rms(mem_ref[...], g_ref[...]).astype(BF16)
    kv = jnp.dot(m, w_ref[...].astype(BF16), preferred_element_type=F32)
    kn = _rms(kv, kg_ref[...]) * (MEM_HEAD_DIM ** -0.5)
    o_ref[...] = jnp.where(pl.program_id(0) < MEM_HEADS, kn, kv).astype(o_ref.dtype)


def _mem_kv(mem, mem_norm_g, w_mem_kv, k_g):
    n = mem.shape[0]
    hd = MEM_HEAD_DIM
    return pl.pallas_call(
        _mem_kv_body,
        out_shape=jax.ShapeDtypeStruct((n, 2 * MEM_HEADS * hd), BF16),
        grid=(2 * MEM_HEADS,),
        in_specs=[
            pl.BlockSpec((n, D_MODEL), lambda j: (0, 0)),
            pl.BlockSpec((1, D_MODEL), lambda j: (0, 0)),
            pl.BlockSpec((D_MODEL, hd), lambda j: (0, j)),
            pl.BlockSpec((1, hd), lambda j: (0, 0)),
        ],
        out_specs=pl.BlockSpec((n, hd), lambda j: (0, j)),
        compiler_params=_params("arbitrary"),
        name="mem_kv",
    )(mem, mem_norm_g, w_mem_kv, k_g)


def _merge_body(x_ref, oda_ref, oret_ref, mq_ref, gates_ref, kv_ref, qg_ref,
                wda_ref, wret_ref, wmem_ref, wout_ref, o_ref, omem_ref):
    hd = MEM_HEAD_DIM
    for h in range(MEM_HEADS):
        q = _rms(mq_ref[:, h * hd:(h + 1) * hd].astype(F32), qg_ref[...]).astype(BF16)
        k = kv_ref[:, h * hd:(h + 1) * hd]
        v = kv_ref[:, (MEM_HEADS + h) * hd:(MEM_HEADS + h + 1) * hd]
        sc = lax.dot_general(q, k, (((1,), (1,)), ((), ())), preferred_element_type=F32)
        p = jnp.exp(sc - jnp.max(sc, axis=-1, keepdims=True))
        o = jnp.dot(p.astype(BF16), v, preferred_element_type=F32) / jnp.sum(p, axis=-1, keepdims=True)
        omem_ref[:, h * hd:(h + 1) * hd] = o.astype(BF16)

    def branch(b, act, w_ref):
        gate = gates_ref[:, b * D_MODEL:(b + 1) * D_MODEL].astype(F32)
        return jax.nn.sigmoid(gate) * jnp.dot(act, w_ref[...], preferred_element_type=F32)

    merged = (branch(0, oda_ref[...], wda_ref) + branch(1, oret_ref[...], wret_ref)
              + branch(2, omem_ref[...], wmem_ref))
    o_ref[...] = x_ref[...] + jnp.dot(merged.astype(BF16), wout_ref[...], preferred_element_type=F32)


def _merge(x, o_da, o_ret, proj, kv_mem, mem_q_g, w_o_da, w_o_ret, w_o_mem, w_out):
    s = x.shape[0]
    tm = MERGE_TM
    n_mem = kv_mem.shape[0]
    bw = 1024
    gw = 3 * D_MODEL
    row = lambda i: (i, 0)
    resident = lambda shape: pl.BlockSpec(shape, lambda i: (0, 0), pipeline_mode=pl.Buffered(1))
    return pl.pallas_call(
        _merge_body,
        out_shape=jax.ShapeDtypeStruct((s, D_MODEL), F32),
        grid=(s // tm,),
        in_specs=[
            pl.BlockSpec((tm, D_MODEL), row),
            pl.BlockSpec((tm, bw), row),
            pl.BlockSpec((tm, bw), row),
            pl.BlockSpec((tm, bw), lambda i: (i, MEM_Q_OFF // bw)),
            pl.BlockSpec((tm, gw), lambda i: (i, GATE_OFF // gw)),
            resident((n_mem, 2 * bw)),
            resident((1, MEM_HEAD_DIM)),
            resident((bw, D_MODEL)),
            resident((bw, D_MODEL)),
            resident((bw, D_MODEL)),
            resident((D_MODEL, D_MODEL)),
        ],
        out_specs=pl.BlockSpec((tm, D_MODEL), row),
        scratch_shapes=[pltpu.VMEM((tm, bw), BF16)],
        compiler_params=_params("arbitrary"),
        name="merge",
    )(x, o_da, o_ret, proj, proj, kv_mem, mem_q_g, w_o_da, w_o_ret, w_o_mem, w_out)


def _ffn_body(x_ref, g_ref, wg_ref, wu_ref, wd_ref, o_ref, h_ref):
    @pl.when(pl.program_id(1) == 0)
    def _():
        x = x_ref[...]
        o_ref[...] = x
        h_ref[...] = _rms(x, g_ref[...]).astype(BF16)

    h = h_ref[...]
    gate = jnp.dot(h, wg_ref[...], preferred_element_type=F32)
    up = jnp.dot(h, wu_ref[...], preferred_element_type=F32)
    act = (gate * jax.nn.sigmoid(gate) * up).astype(BF16)
    o_ref[...] += jnp.dot(act, wd_ref[...], preferred_element_type=F32)


def _ffn(x, g, w_gate, w_up, w_down):
    s = x.shape[0]
    tm, tf = FFN_TM, FFN_TF
    return pl.pallas_call(
        _ffn_body,
        out_shape=jax.ShapeDtypeStruct((s, D_MODEL), F32),
        grid=(s // tm, D_FF // tf),
        in_specs=[
            pl.BlockSpec((tm, D_MODEL), lambda i, j: (i, 0)),
            pl.BlockSpec((1, D_MODEL), lambda i, j: (0, 0)),
            pl.BlockSpec((D_MODEL, tf), lambda i, j: (0, j)),
            pl.BlockSpec((D_MODEL, tf), lambda i, j: (0, j)),
            pl.BlockSpec((tf, D_MODEL), lambda i, j: (j, 0)),
        ],
        out_specs=pl.BlockSpec((tm, D_MODEL), lambda i, j: (i, 0)),
        scratch_shapes=[pltpu.VMEM((tm, D_MODEL), BF16)],
        compiler_params=_params("arbitrary", "arbitrary"),
        name="ffn",
    )(x, g, w_gate, w_up, w_down)


def _layer(x, mem, positions, l, attn_norm_g, w_in, da_q_norm_g, da_k_norm_g, da_lambda_q1, da_lambda_k1,
           da_lambda_q2, da_lambda_k2, da_subln_g, ret_norm_g, mem_norm_g, w_mem_kv, mem_q_norm_g,
           mem_k_norm_g, w_o_da, w_o_ret, w_o_mem, w_out, ffn_norm_g, w_ffn_gate, w_ffn_up, w_ffn_down):
    lambda_init = 0.8 - 0.6 * math.exp(-0.3 * l)
    row = lambda a: a.astype(F32).reshape(1, -1)
    proj = _in_proj(x, row(attn_norm_g), w_in.astype(BF16))
    da_qt, da_k, da_vt, rq, rkt = _qk_prep(proj, positions, da_q_norm_g, da_k_norm_g)
    lam_params = jnp.stack([da_lambda_q1, da_lambda_k1, da_lambda_q2, da_lambda_k2]).astype(F32)
    o_da = _diff_attn(da_qt, da_k, da_vt, lam_params, row(da_subln_g), lambda_init)
    o_ret = _retention(rq, rkt, proj, row(ret_norm_g))
    kv_mem = _mem_kv(mem, row(mem_norm_g), w_mem_kv, row(mem_k_norm_g))
    x1 = _merge(x, o_da, o_ret, proj, kv_mem, row(mem_q_norm_g), w_o_da.astype(BF16),
                w_o_ret.astype(BF16), w_o_mem.astype(BF16), w_out.astype(BF16))
    return _ffn(x1, row(ffn_norm_g), w_ffn_gate.astype(BF16), w_ffn_up.astype(BF16),
                w_ffn_down.astype(BF16))


def kernel(x, mem, positions, attn_norm_g, w_in, da_q_norm_g, da_k_norm_g, da_lambda_q1, da_lambda_k1,
           da_lambda_q2, da_lambda_k2, da_subln_g, ret_norm_g, mem_norm_g, w_mem_kv, mem_q_norm_g,
           mem_k_norm_g, w_o_da, w_o_ret, w_o_mem, w_out, ffn_norm_g, w_ffn_gate, w_ffn_up, w_ffn_down):
    batch, depth = x.shape[0], w_in.shape[0]
    outs = []
    for b in range(batch):
        xb = x[b]
        for l in range(depth):
            xb = _layer(xb, mem[b], positions[b], l, attn_norm_g[l], w_in[l], da_q_norm_g[l], da_k_norm_g[l],
                        da_lambda_q1[l], da_lambda_k1[l], da_lambda_q2[l], da_lambda_k2[l], da_subln_g[l],
                        ret_norm_g[l], mem_norm_g[l], w_mem_kv[l], mem_q_norm_g[l], mem_k_norm_g[l],
                        w_o_da[l], w_o_ret[l], w_o_mem[l], w_out[l], ffn_norm_g[l], w_ffn_gate[l],
                        w_ffn_up[l], w_ffn_down[l])
        outs.append(xb)
    return jnp.stack(outs)
```
